```python
import jax, jax.numpy as jnp
from jax import lax
import numpy as np

D_MODEL = 1024
BATCH = 4
SEQ = 4096
DEPTH = 2

GRID_W = 64
CTX_LEN = 256
GROUP_W = D_MODEL // 4
HEAD_DIM = 64
N_HEADS_GROUP = GROUP_W // HEAD_DIM
SC_KERNEL = 3
MLA_Q_RANK = 256
MLA_KV_RANK = 128
MLA_NOPE = 64
MLA_ROPE = 32
MLA_V = 64
MLA_SCALE = (MLA_NOPE + MLA_ROPE) ** -0.5
Q_BLOCK = 128
CF_KERNEL = 31
NA_ROWS = 8
NA_COLS = 16
NA_SCALE = HEAD_DIM ** -0.5
ROPE_BASE = 10000.0
MLP_HIDDEN = 4 * D_MODEL
N_MOD = 6
EPS = 1e-6

SC_END = 3 * GROUP_W
MLA_Q_END = SC_END + MLA_Q_RANK
MLA_KV_END = MLA_Q_END + MLA_KV_RANK + MLA_ROPE
CF_END = MLA_KV_END + 2 * GROUP_W
NA_Q_END = CF_END + GROUP_W
P_IN = NA_Q_END + 2 * GROUP_W

kernel_name = "hybrid_parallel_groups_dit_block"


def rmsnorm(x, g):
    xf = x.astype(jnp.float32)
    y = xf * lax.rsqrt(jnp.mean(xf * xf, axis=-1, keepdims=True) + EPS)
    return y.astype(x.dtype) * g


def layernorm(x, g, b):
    xf = x.astype(jnp.float32)
    mu = jnp.mean(xf, axis=-1, keepdims=True)
    var = jnp.mean(jnp.square(xf - mu), axis=-1, keepdims=True)
    return ((xf - mu) * lax.rsqrt(var + EPS)).astype(x.dtype) * g + b


def dwconv(x, w):
    k = w.shape[0]
    return lax.conv_general_dilated(
        x, w[:, None, :], window_strides=(1,), padding=[(k // 2, k // 2)],
        dimension_numbers=("NWC", "WIO", "NWC"), feature_group_count=x.shape[-1])


def heads(t):
    return t.reshape(*t.shape[:-1], N_HEADS_GROUP, -1)


def rope_rotate(x, ang):
    x1, x2 = jnp.split(x, 2, axis=-1)
    cos, sin = jnp.cos(ang).astype(x.dtype), jnp.sin(ang).astype(x.dtype)
    return jnp.concatenate([x1 * cos - x2 * sin, x1 * sin + x2 * cos], axis=-1)


def axial_rope(x, ang_row, ang_col):
    xr, xc = jnp.split(x, 2, axis=-1)
    return jnp.concatenate([rope_rotate(xr, ang_row), rope_rotate(xc, ang_col)], axis=-1)


def short_conv_mixer(u_sc, w_sc):
    x_in, b_gate, c_gate = jnp.split(u_sc, 3, axis=-1)
    return b_gate * dwconv(c_gate * x_in, w_sc)


def conformer_conv_mixer(u_cf, w_dw, b_dw, ln_g, ln_b, w_pw):
    a, g = jnp.split(u_cf, 2, axis=-1)
    y = dwconv(a * jax.nn.sigmoid(g), w_dw) + b_dw
    return jax.nn.silu(layernorm(y, ln_g, ln_b)) @ w_pw


def mla_queries(cq, g_q, w_uq):
    q = heads(rmsnorm(cq, g_q) @ w_uq)
    return q[..., :MLA_NOPE], q[..., MLA_NOPE:]


def mla_keys_values(u_kv, g_kv, w_ukv):
    ckv, k_rope = u_kv[..., :MLA_KV_RANK], u_kv[..., MLA_KV_RANK:]
    kv = heads(rmsnorm(ckv, g_kv) @ w_ukv)
    return kv[..., :MLA_NOPE], k_rope, kv[..., MLA_NOPE:]


def mla_attend(qn, qr, kn, kr, v):
    s = (jnp.einsum("bqhd,bkhd->bhqk", qn, kn)
         + jnp.einsum("bqhr,bkr->bhqk", qr, kr)) * MLA_SCALE
    p = jax.nn.softmax(s.astype(jnp.float32), axis=-1).astype(v.dtype)
    return jnp.einsum("bhqk,bkhd->bqhd", p, v)


def mla_latent(qn, qr, kn, kr, v):
    b, l = qn.shape[:2]
    nb = l // Q_BLOCK

    def to_blocks(t):
        return jnp.moveaxis(t.reshape(b, nb, Q_BLOCK, *t.shape[2:]), 1, 0)

    out = lax.map(lambda qs: mla_attend(qs[0], qs[1], kn, kr, v), (to_blocks(qn), to_blocks(qr)))
    return jnp.moveaxis(out, 0, 1).reshape(b, l, -1)


def dense_attend(q, k, v):
    s = jnp.einsum("bqhd,bkhd->bhqk", q, k) * NA_SCALE
    p = jax.nn.softmax(s.astype(jnp.float32), axis=-1).astype(v.dtype)
    o = jnp.einsum("bhqk,bkhd->bqhd", p, v)
    return o.reshape(*o.shape[:2], -1)


def neighbourhood_attend(q, k, v, k_c, v_c, rpb):
    b, l, h, dh = q.shape
    rows = l // GRID_W
    wh = min(NA_ROWS, rows)
    r = jnp.arange(rows)
    row_idx = jnp.clip(r - wh // 2, 0, rows - wh)[:, None] + jnp.arange(wh)[None, :]
    col = jnp.arange(GRID_W)
    col_start = jnp.clip(col - NA_COLS // 2, 0, GRID_W - NA_COLS)
    in_win = (col[None, :] >= col_start[:, None]) & (col[None, :] < col_start[:, None] + NA_COLS)
    off_r = row_idx - r[:, None] + (NA_ROWS - 1)
    off_c = jnp.clip(col[None, :] - col[:, None] + (NA_COLS - 1), 0, 2 * NA_COLS - 2)
    bias = rpb[:, off_r[:, None, :, None], off_c[None, :, None, :]]

    qg = q.reshape(b, rows, GRID_W, h, dh)
    kw = k.reshape(b, rows, GRID_W, h, dh)[:, row_idx]
    vw = v.reshape(b, rows, GRID_W, h, dh)[:, row_idx]
    s_loc = (jnp.einsum("brqhd,brikhd->bhrqik", qg, kw).astype(jnp.float32) * NA_SCALE
             + bias.astype(jnp.float32)[None])
    s_loc = jnp.where(in_win[:, None, :], s_loc, -jnp.inf)
    s_ctx = jnp.einsum("brqhd,bchd->bhrqc", qg, k_c).astype(jnp.float32) * NA_SCALE
    n_loc = wh * GRID_W
    p = jax.nn.softmax(
        jnp.concatenate([s_loc.reshape(b, h, rows, GRID_W, n_loc), s_ctx], axis=-1),
        axis=-1).astype(v.dtype)
    out = (jnp.einsum("bhrqik,brikhd->brqhd",
                      p[..., :n_loc].reshape(b, h, rows, GRID_W, wh, GRID_W), vw)
           + jnp.einsum("bhrqc,bchd->brqhd", p[..., n_loc:], v_c))
    return out.reshape(b, l, h * dh)


def sq_relu_mlp(h, w1, w2):
    return jnp.square(jax.nn.relu(h @ w1)) @ w2


def trunk_layer(x, xc, mod, mod_c, p, ang_row, ang_col, last):
    sh1, sc1, ga1, sh2, sc2, ga2 = mod
    sh1c, sc1c, ga1c, sh2c, sc2c, ga2c = mod_c
    w_in = p["w_in"]

    h = rmsnorm(x, p["g_pre_mix"]) * (1 + sc1) + sh1
    hc = rmsnorm(xc, p["g_pre_mix"]) * (1 + sc1c) + sh1c
    u = h @ w_in
    if last:
        uc_mla_kv = hc @ w_in[:, MLA_Q_END:MLA_KV_END]
        uc_na_kv = hc @ w_in[:, NA_Q_END:]
    else:
        uc = hc @ w_in
        uc_mla_kv, uc_na_kv = uc[..., MLA_Q_END:MLA_KV_END], uc[..., NA_Q_END:]

    kn_c, kr_c, v_c = mla_keys_values(uc_mla_kv, p["g_kv"], p["w_ukv"])
    nk_c, nv_c = (heads(t) for t in jnp.split(uc_na_kv, 2, axis=-1))

    y_sc = short_conv_mixer(u[..., :SC_END], p["w_sc"])
    qn, qr = mla_queries(u[..., SC_END:MLA_Q_END], p["g_q"], p["w_uq"])
    qr = axial_rope(qr, ang_row[:, None], ang_col[:, None])
    kn, kr, v = mla_keys_values(u[..., MLA_Q_END:MLA_KV_END], p["g_kv"], p["w_ukv"])
    kr = axial_rope(kr, ang_row, ang_col)
    y_mla = mla_latent(qn, qr, jnp.concatenate([kn, kn_c], axis=1),
                       jnp.concatenate([kr, kr_c], axis=1), jnp.concatenate([v, v_c], axis=1))
    y_cf = conformer_conv_mixer(u[..., MLA_KV_END:CF_END], p["cf_w_dw"], p["cf_b_dw"],
                                p["cf_ln_g"], p["cf_ln_b"], p["cf_w_pw"])
    nq, nk, nv = (heads(t) for t in jnp.split(u[..., CF_END:], 3, axis=-1))
    y_na = neighbourhood_attend(nq, nk, nv, nk_c, nv_c, p["na_rpb"])
    y = jnp.concatenate([y_sc, y_mla, y_cf, y_na], axis=-1) @ p["w_out"]
    x = x + ga1 * rmsnorm(y, p["g_post_mix"])

    if not last:
        yc_sc = short_conv_mixer(uc[..., :SC_END], p["w_sc"])
        qn_c, qr_c = mla_queries(uc[..., SC_END:MLA_Q_END], p["g_q"], p["w_uq"])
        yc_mla = mla_attend(qn_c, qr_c, kn_c, kr_c, v_c)
        yc_mla = yc_mla.reshape(*yc_mla.shape[:2], -1)
        yc_cf = conformer_conv_mixer(uc[..., MLA_KV_END:CF_END], p["cf_w_dw"], p["cf_b_dw"],
                                     p["cf_ln_g"], p["cf_ln_b"], p["cf_w_pw"])
        yc_na = dense_attend(heads(uc[..., CF_END:NA_Q_END]), nk_c, nv_c)
        yc = jnp.concatenate([yc_sc, yc_mla, yc_cf, yc_na], axis=-1) @ p["w_out"]
        xc = xc + ga1c * rmsnorm(yc, p["g_post_mix"])

    hm = rmsnorm(x, p["g_pre_mlp"]) * (1 + sc2) + sh2
    x = x + ga2 * rmsnorm(sq_relu_mlp(hm, p["w_mlp1"], p["w_mlp2"]), p["g_post_mlp"])
    if not last:
        hmc = rmsnorm(xc, p["g_pre_mlp"]) * (1 + sc2c) + sh2c
        xc = xc + ga2c * rmsnorm(sq_relu_mlp(hmc, p["w_mlp1"], p["w_mlp2"]), p["g_post_mlp"])
    return x, xc


def setup_inputs(seed: int = 0) -> dict:
    key = jax.random.key(seed)
    ks = jax.random.split(key, 25)
    f32 = jnp.float32

    def nrm(k, shape, s):
        return jax.random.normal(k, shape, f32) * s

    def gain(k, shape):
        return 1.0 + 0.1 * jax.random.normal(k, shape, f32)

    L, H = DEPTH, N_HEADS_GROUP
    return {
        "x": nrm(ks[0], (BATCH, SEQ, D_MODEL), 1.0),
        "c": nrm(ks[1], (BATCH, D_MODEL), 1.0),
        "ctx": nrm(ks[2], (BATCH, CTX_LEN, D_MODEL), 1.0),
        "c_ctx": nrm(ks[3], (D_MODEL,), 1.0),
        "w_mod": nrm(ks[4], (L, D_MODEL, N_MOD * D_MODEL), 0.5 * D_MODEL ** -0.5),
        "b_mod": nrm(ks[5], (L, N_MOD * D_MODEL), 0.02),
        "g_pre_mix": gain(ks[6], (L, D_MODEL)),
        "w_in": nrm(ks[7], (L, D_MODEL, P_IN), D_MODEL ** -0.5),
        "w_sc": nrm(ks[8], (L, SC_KERNEL, GROUP_W), SC_KERNEL ** -0.5),
        "g_q": gain(ks[9], (L, MLA_Q_RANK)),
        "w_uq": nrm(ks[10], (L, MLA_Q_RANK, H * (MLA_NOPE + MLA_ROPE)), MLA_Q_RANK ** -0.5),
        "g_kv": gain(ks[11], (L, MLA_KV_RANK)),
        "w_ukv": nrm(ks[12], (L, MLA_KV_RANK, H * (MLA_NOPE + MLA_V)), MLA_KV_RANK ** -0.5),
        "cf_w_dw": nrm(ks[13], (L, CF_KERNEL, GROUP_W), CF_KERNEL ** -0.5),
        "cf_b_dw": nrm(ks[14], (L, GROUP_W), 0.02),
        "cf_ln_g": gain(ks[15], (L, GROUP_W)),
        "cf_ln_b": nrm(ks[16], (L, GROUP_W), 0.02),
        "cf_w_pw": nrm(ks[17], (L, GROUP_W, GROUP_W), GROUP_W ** -0.5),
        "na_rpb": nrm(ks[18], (L, H, 2 * NA_ROWS - 1, 2 * NA_COLS - 1), 0.1),
        "w_out": nrm(ks[19], (L, D_MODEL, D_MODEL), D_MODEL ** -0.5),
        "g_post_mix": gain(ks[20], (L, D_MODEL)),
        "g_pre_mlp": gain(ks[21], (L, D_MODEL)),
        "w_mlp1": nrm(ks[22], (L, D_MODEL, MLP_HIDDEN), D_MODEL ** -0.5),
        "w_mlp2": nrm(ks[23], (L, MLP_HIDDEN, D_MODEL), MLP_HIDDEN ** -0.5),
        "g_post_mlp": gain(ks[24], (L, D_MODEL)),
    }


def reference(x, c, ctx, c_ctx, w_mod, b_mod, g_pre_mix, w_in, w_sc, g_q, w_uq, g_kv, w_ukv,
              cf_w_dw, cf_b_dw, cf_ln_g, cf_ln_b, cf_w_pw, na_rpb, w_out, g_post_mix,
              g_pre_mlp, w_mlp1, w_mlp2, g_post_mlp):
    l = x.shape[1]
    pos = jnp.arange(l)
    n_freq = MLA_ROPE // 4
    inv_freq = ROPE_BASE ** (-jnp.arange(n_freq, dtype=jnp.float32) / n_freq)
    ang_row = (pos // GRID_W).astype(jnp.float32)[:, None] * inv_freq
    ang_col = (pos % GRID_W).astype(jnp.float32)[:, None] * inv_freq

    xc = ctx
    for i in range(DEPTH):
        mod = [m[:, None, :] for m in jnp.split(jax.nn.silu(c) @ w_mod[i] + b_mod[i], N_MOD, axis=-1)]
        mod_c = jnp.split(jax.nn.silu(c_ctx) @ w_mod[i] + b_mod[i], N_MOD, axis=-1)
        p = {
            "g_pre_mix": g_pre_mix[i], "w_in": w_in[i], "w_sc": w_sc[i],
            "g_q": g_q[i], "w_uq": w_uq[i], "g_kv": g_kv[i], "w_ukv": w_ukv[i],
            "cf_w_dw": cf_w_dw[i], "cf_b_dw": cf_b_dw[i], "cf_ln_g": cf_ln_g[i],
            "cf_ln_b": cf_ln_b[i], "cf_w_pw": cf_w_pw[i], "na_rpb": na_rpb[i],
            "w_out": w_out[i], "g_post_mix": g_post_mix[i], "g_pre_mlp": g_pre_mlp[i],
            "w_mlp1": w_mlp1[i], "w_mlp2": w_mlp2[i], "g_post_mlp": g_post_mlp[i],
        }
        x, xc = trunk_layer(x, xc, mod, mod_c, p, ang_row, ang_col, last=(i == DEPTH - 1))
    return x
```

```python
import functools

import jax
import jax.numpy as jnp
from jax import lax
from jax.experimental import pallas as pl
from jax.experimental.pallas import tpu as pltpu

D_MODEL = 1024
GRID_W = 64
GROUP_W = D_MODEL // 4
HEAD_DIM = 64
N_HEADS = GROUP_W // HEAD_DIM
SC_KERNEL = 3
MLA_Q_RANK = 256
MLA_KV_RANK = 128
MLA_NOPE = 64
MLA_ROPE = 32
MLA_V = 64
MLA_SCALE = (MLA_NOPE + MLA_ROPE) ** -0.5
CF_KERNEL = 31
NA_ROWS = 8
NA_COLS = 16
NA_SCALE = HEAD_DIM ** -0.5
ROPE_BASE = 10000.0
MLP_HIDDEN = 4 * D_MODEL
N_MOD = 6
EPS = 1e-6

SC_END = 3 * GROUP_W
MLA_Q_END = SC_END + MLA_Q_RANK
MLA_KV_END = MLA_Q_END + MLA_KV_RANK + MLA_ROPE
CF_END = MLA_KV_END + 2 * GROUP_W
NA_Q_END = CF_END + GROUP_W
P_IN = NA_Q_END + 2 * GROUP_W

LANES = 128
HEAD_PAD = LANES
MLA_PAD = N_HEADS * HEAD_PAD
PC_SC = 0
PC_CQ = PC_SC + SC_END
PC_CKV = PC_CQ + MLA_Q_RANK
PC_ROPE = PC_CKV + MLA_KV_RANK
PC_CF = PC_ROPE + LANES
PC_NA = PC_CF + 2 * GROUP_W
P_PAD = PC_NA + 3 * GROUP_W

NA_QROWS = 4
NA_WROWS = NA_QROWS + NA_ROWS
NA_TQ = NA_QROWS * GRID_W
NA_TK = NA_WROWS * GRID_W
HALO = 16
MASK_VALUE = -1e30

VMEM_LIMIT = 56 * 1024 * 1024


def _cparams(n_axes):
    return pltpu.CompilerParams(dimension_semantics=("arbitrary",) * n_axes,
                                vmem_limit_bytes=VMEM_LIMIT)


def _const_spec(shape):
    nd = len(shape)
    return pl.BlockSpec(shape, lambda *_: (0,) * nd)


def _rms(x):
    return x * lax.rsqrt(jnp.mean(x * x, axis=-1, keepdims=True) + EPS)


def _dot(a, b):
    return jnp.dot(a, b, preferred_element_type=jnp.float32)


def _dot_nt(a, b):
    return lax.dot_general(a, b, (((1,), (1,)), ((), ())), preferred_element_type=jnp.float32)


def _mod_kernel(cs_ref, w_ref, b_ref, o_ref):
    cs = cs_ref[...]
    a = (cs * jax.nn.sigmoid(cs)).astype(jnp.bfloat16)
    o_ref[0] = _dot(a, w_ref[0].astype(jnp.bfloat16)) + b_ref[0]


def _modulation(cs, w_mod, b_mod):
    depth, d, n = w_mod.shape
    bn = 1024
    return pl.pallas_call(
        _mod_kernel,
        grid=(depth, n // bn),
        in_specs=[pl.BlockSpec(cs.shape, lambda l, j: (0, 0)),
                  pl.BlockSpec((1, d, bn), lambda l, j: (l, 0, j)),
                  pl.BlockSpec((1, 1, bn), lambda l, j: (l, 0, j))],
        out_specs=pl.BlockSpec((1, cs.shape[0], bn), lambda l, j: (l, 0, j)),
        out_shape=jax.ShapeDtypeStruct((depth, cs.shape[0], n), jnp.float32),
        compiler_params=_cparams(2), name="modulation",
    )(cs, w_mod, b_mod.reshape(depth, 1, n))


def _inproj_kernel(x_ref, mod_ref, gpre_ref, win_ref, gq_ref, wuq_ref, gkv_ref, wukv_ref,
                   cq_ref, sq_ref, tk_ref,
                   usc_ref, q_ref, k_ref, v_ref, ucf_ref, nq_ref, nk_ref, nv_ref):
    x = x_ref[0]
    sh1 = mod_ref[0, 0:1, :]
    sc1 = mod_ref[0, 1:2, :]
    h = (_rms(x) * gpre_ref[...]) * (1.0 + sc1) + sh1
    u = _dot(h.astype(jnp.bfloat16), win_ref[...])

    usc_ref[0] = u[:, PC_SC:PC_CQ]
    ucf_ref[0] = u[:, PC_CF:PC_NA]
    nq_ref[0] = u[:, PC_NA:PC_NA + GROUP_W].astype(jnp.bfloat16)
    nk_ref[0] = u[:, PC_NA + GROUP_W:PC_NA + 2 * GROUP_W].astype(jnp.bfloat16)
    nv_ref[0] = u[:, PC_NA + 2 * GROUP_W:P_PAD].astype(jnp.bfloat16)

    cqn = (_rms(u[:, PC_CQ:PC_CKV]) * gq_ref[...]).astype(jnp.bfloat16)
    qf = _dot(cqn, wuq_ref[...])
    cq_t = cq_ref[...]
    sq_t = sq_ref[...]
    for hd in range(N_HEADS):
        lo = hd * HEAD_PAD
        qh = qf[:, lo:lo + HEAD_PAD] * cq_t + qf[:, MLA_PAD + lo:MLA_PAD + lo + HEAD_PAD] * sq_t
        q_ref[0, :, lo:lo + HEAD_PAD] = qh.astype(jnp.bfloat16)

    ckvn = (_rms(u[:, PC_CKV:PC_ROPE]) * gkv_ref[...]).astype(jnp.bfloat16)
    kvf = _dot(ckvn, wukv_ref[...])
    g = u[:, PC_ROPE:PC_CF] * tk_ref[...]
    lane = lax.broadcasted_iota(jnp.int32, g.shape, 1)
    rot = pltpu.roll(g, 2 * MLA_ROPE, axis=1) + g
    kr = jnp.where((lane >= MLA_NOPE) & (lane < MLA_NOPE + MLA_ROPE), rot, 0.0)
    for hd in range(N_HEADS):
        lo = hd * HEAD_PAD
        k_ref[0, :, lo:lo + HEAD_PAD] = (kvf[:, lo:lo + HEAD_PAD] + kr).astype(jnp.bfloat16)
        vh = kvf[:, MLA_PAD + lo:MLA_PAD + lo + HEAD_PAD]
        v_ref[0, :, lo:lo + HEAD_PAD] = jnp.where(lane == MLA_V, 1.0, vh).astype(jnp.bfloat16)


def _inproj(x, mod, lw, tabs, tm):
    b, l, d = x.shape
    bm = mod.shape[0]
    grid = (b, l // tm)
    tok = lambda w: pl.BlockSpec((1, tm, w), lambda i, j: (i, j, 0))
    tab = pl.BlockSpec((tm, LANES), lambda i, j: (j, 0))
    f32, bf16 = jnp.float32, jnp.bfloat16
    out_shape = [jax.ShapeDtypeStruct((b, l, SC_END), f32),
                 jax.ShapeDtypeStruct((b, l, MLA_PAD), bf16),
                 jax.ShapeDtypeStruct((b, l, MLA_PAD), bf16),
                 jax.ShapeDtypeStruct((b, l, MLA_PAD), bf16),
                 jax.ShapeDtypeStruct((b, l, 2 * GROUP_W), f32),
                 jax.ShapeDtypeStruct((b, l, GROUP_W), bf16),
                 jax.ShapeDtypeStruct((b, l, GROUP_W), bf16),
                 jax.ShapeDtypeStruct((b, l, GROUP_W), bf16)]
    return pl.pallas_call(
        _inproj_kernel,
        grid=grid,
        in_specs=[tok(d),
                  pl.BlockSpec((1, N_MOD, d), (lambda i, j: (i, 0, 0)) if bm > 1 else (lambda i, j: (0, 0, 0))),
                  _const_spec((1, d)),
                  _const_spec((d, P_PAD)),
                  _const_spec((1, MLA_Q_RANK)),
                  _const_spec((MLA_Q_RANK, 2 * MLA_PAD)),
                  _const_spec((1, MLA_KV_RANK)),
                  _const_spec((MLA_KV_RANK, 2 * MLA_PAD)),
                  tab, tab, tab],
        out_specs=[tok(SC_END), tok(MLA_PAD), tok(MLA_PAD), tok(MLA_PAD), tok(2 * GROUP_W),
                   tok(GROUP_W), tok(GROUP_W), tok(GROUP_W)],
        out_shape=out_shape,
        compiler_params=_cparams(2), name="inproj",
    )(x, mod, lw["g_pre_mix"], lw["w_in"], lw["g_q"], lw["w_uq"], lw["g_kv"], lw["w_ukv"],
      tabs["cq"], tabs["sq"], tabs["tk"])


def _mla_kernel(*refs, has_lat):
    if has_lat:
        q_ref, kl_ref, vl_ref, kc_ref, vc_ref, o_ref = refs
    else:
        q_ref, kc_ref, vc_ref, o_ref = refs
    q = q_ref[0]
    s_c = _dot_nt(q, kc_ref[0])
    m = jnp.max(s_c, axis=-1, keepdims=True)
    if has_lat:
        s_l = _dot_nt(q, kl_ref[0])
        m = jnp.maximum(m, jnp.max(s_l, axis=-1, keepdims=True))
    p_c = jnp.exp((s_c - m) * MLA_SCALE).astype(jnp.bfloat16)
    o = _dot(p_c, vc_ref[0])
    if has_lat:
        p_l = jnp.exp((s_l - m) * MLA_SCALE).astype(jnp.bfloat16)
        o = o + _dot(p_l, vl_ref[0])
    o_ref[0] = (o * (1.0 / o[:, MLA_V:MLA_V + 1])).astype(jnp.bfloat16)


def _mla(q, k_lat, v_lat, k_ctx, v_ctx, tq):
    b, lq, _ = q.shape
    lc = k_ctx.shape[1]
    has_lat = k_lat is not None
    grid = (b, N_HEADS, lq // tq)
    qspec = pl.BlockSpec((1, tq, HEAD_PAD), lambda i, h, j: (i, j, h))
    kvspec = lambda n: pl.BlockSpec((1, n, HEAD_PAD), lambda i, h, j: (i, 0, h))
    in_specs = [qspec]
    args = [q]
    if has_lat:
        in_specs += [kvspec(k_lat.shape[1])] * 2
        args += [k_lat, v_lat]
    in_specs += [kvspec(lc)] * 2
    args += [k_ctx, v_ctx]
    return pl.pallas_call(
        functools.partial(_mla_kernel, has_lat=has_lat),
        grid=grid,
        in_specs=in_specs,
        out_specs=qspec,
        out_shape=jax.ShapeDtypeStruct((b, lq, MLA_PAD), jnp.bfloat16),
        compiler_params=_cparams(3), name="mla_attn",
    )(*args)


def _na_kernel(*refs, has_local):
    if has_local:
        q_ref, k_ref, v_ref, kc_ref, vc_ref, bias_ref, o_ref = refs
        blk = pl.program_id(1)
        nblk = pl.num_programs(1)
        wrow = jnp.clip(blk * NA_QROWS - NA_ROWS // 2, 0, GRID_W - NA_WROWS)
        start = pl.multiple_of(wrow * GRID_W, GRID_W)
        case = jnp.where(blk == 0, 0, jnp.where(blk == nblk - 1, 2, 1))
        kwin = k_ref[0, pl.ds(start, NA_TK), :]
        vwin = v_ref[0, pl.ds(start, NA_TK), :]
    else:
        q_ref, kc_ref, vc_ref, o_ref = refs
    q = q_ref[0]
    kc = kc_ref[0]
    vc = vc_ref[0]
    lane = lax.broadcasted_iota(jnp.int32, q.shape, 1)
    out = jnp.zeros(q.shape, jnp.float32)
    for hd in range(N_HEADS):
        in_head = (lane >= hd * HEAD_DIM) & (lane < (hd + 1) * HEAD_DIM)
        qm = jnp.where(in_head, q, jnp.zeros_like(q))
        s_c = _dot_nt(qm, kc) * NA_SCALE
        m = jnp.max(s_c, axis=-1, keepdims=True)
        if has_local:
            s_l = _dot_nt(qm, kwin) * NA_SCALE + bias_ref[case, hd]
            m = jnp.maximum(m, jnp.max(s_l, axis=-1, keepdims=True))
        p_c = jnp.exp(s_c - m)
        den = jnp.sum(p_c, axis=-1, keepdims=True)
        o = _dot(p_c.astype(jnp.bfloat16), vc)
        if has_local:
            p_l = jnp.exp(s_l - m)
            den = den + jnp.sum(p_l, axis=-1, keepdims=True)
            o = o + _dot(p_l.astype(jnp.bfloat16), vwin)
        out = jnp.where(in_head, o * (1.0 / den), out)
    o_ref[0] = out.astype(jnp.bfloat16)


def _na(q, k, v, k_ctx, v_ctx, bias):
    b, l, w = q.shape
    lc = k_ctx.shape[1]
    has_local = k is not None
    tq = NA_TQ
    grid = (b, l // tq)
    qspec = pl.BlockSpec((1, tq, w), lambda i, j: (i, j, 0))
    full = lambda n: pl.BlockSpec((1, n, w), lambda i, j: (i, 0, 0))
    in_specs = [qspec]
    args = [q]
    if has_local:
        in_specs += [full(l), full(l)]
        args += [k, v]
    in_specs += [full(lc), full(lc)]
    args += [k_ctx, v_ctx]
    if has_local:
        in_specs += [_const_spec(bias.shape)]
        args += [bias]
    return pl.pallas_call(
        functools.partial(_na_kernel, has_local=has_local),
        grid=grid,
        in_specs=in_specs,
        out_specs=qspec,
        out_shape=jax.ShapeDtypeStruct((b, l, w), jnp.bfloat16),
        compiler_params=_cparams(2), name="na_attn",
    )(*args)


def _na_bias(rpb):
    rows = GRID_W
    r0 = jnp.array([0, NA_QROWS, rows - NA_QROWS])[:, None, None]
    wrow = jnp.clip(r0 - NA_ROWS // 2, 0, rows - NA_WROWS)
    r = r0 + jnp.arange(NA_QROWS)[None, :, None]
    kr = wrow + jnp.arange(NA_WROWS)[None, None, :]
    band = jnp.clip(r - NA_ROWS // 2, 0, rows - NA_ROWS)
    row_ok = (kr >= band) & (kr < band + NA_ROWS)
    off_r = jnp.clip(kr - r + (NA_ROWS - 1), 0, 2 * NA_ROWS - 2)
    col = jnp.arange(GRID_W)
    cstart = jnp.clip(col - NA_COLS // 2, 0, GRID_W - NA_COLS)
    col_ok = (col[None, :] >= cstart[:, None]) & (col[None, :] < cstart[:, None] + NA_COLS)
    off_c = jnp.clip(col[None, :] - col[:, None] + (NA_COLS - 1), 0, 2 * NA_COLS - 2)
    vals = rpb[:, off_r[:, :, None, :, None], off_c[None, None, :, None, :]]
    ok = row_ok[:, :, None, :, None] & col_ok[None, None, :, None, :]
    bias = jnp.where(ok[None], vals, MASK_VALUE)
    bias = jnp.moveaxis(bias, 0, 1)
    return bias.reshape(3, N_HEADS, NA_TQ, NA_TK).astype(jnp.float32)


def _local_kernel(usc_ref, uscp_ref, uscn_ref, ucf_ref, ucfp_ref, ucfn_ref,
                  wsc_ref, wdw_ref, bdw_ref, lng_ref, lnb_ref, wpw_ref,
                  o_ref, ext_sc, ext_cf, *, t_len):
    j = pl.program_id(1)
    prev_ok = j > 0
    next_ok = j < pl.num_programs(1) - 1

    def gated_sc(u):
        return u[:, 2 * GROUP_W:3 * GROUP_W] * u[:, 0:GROUP_W]

    def glu(u):
        return u[:, 0:GROUP_W] * jax.nn.sigmoid(u[:, GROUP_W:2 * GROUP_W])

    usc = usc_ref[0]
    ext_sc[0:HALO, :] = jnp.where(prev_ok, gated_sc(uscp_ref[0]), 0.0)
    ext_sc[HALO:HALO + t_len, :] = gated_sc(usc)
    ext_sc[HALO + t_len:2 * HALO + t_len, :] = jnp.where(next_ok, gated_sc(uscn_ref[0]), 0.0)
    ext_cf[0:HALO, :] = jnp.where(prev_ok, glu(ucfp_ref[0]), 0.0)
    ext_cf[HALO:HALO + t_len, :] = glu(ucf_ref[0])
    ext_cf[HALO + t_len:2 * HALO + t_len, :] = jnp.where(next_ok, glu(ucfn_ref[0]), 0.0)

    acc = jnp.zeros((t_len, GROUP_W), jnp.float32)
    for k in range(SC_KERNEL):
        off = HALO - SC_KERNEL // 2 + k
        acc = acc + ext_sc[off:off + t_len, :] * wsc_ref[k:k + 1, :]
    y_sc = usc[:, GROUP_W:2 * GROUP_W] * acc

    acc = jnp.zeros((t_len, GROUP_W), jnp.float32)
    for k in range(CF_KERNEL):
        off = HALO - CF_KERNEL // 2 + k
        acc = acc + ext_cf[off:off + t_len, :] * wdw_ref[k:k + 1, :]
    y = acc + bdw_ref[...]
    mu = jnp.mean(y, axis=-1, keepdims=True)
    yc = y - mu
    var = jnp.mean(yc * yc, axis=-1, keepdims=True)
    z = (yc * lax.rsqrt(var + EPS)) * lng_ref[...] + lnb_ref[...]
    z = z * jax.nn.sigmoid(z)
    y_cf = _dot(z.astype(jnp.bfloat16), wpw_ref[...])

    o_ref[0, :, 0:GROUP_W] = y_sc.astype(jnp.bfloat16)
    o_ref[0, :, GROUP_W:2 * GROUP_W] = y_cf.astype(jnp.bfloat16)


def _local(usc, ucf, lw, t_len):
    b, l, _ = usc.shape
    nt = l // t_len
    hb = t_len // HALO
    nhb = l // HALO
    grid = (b, nt)
    main = lambda w: pl.BlockSpec((1, t_len, w), lambda i, j: (i, j, 0))
    prev = lambda w: pl.BlockSpec((1, HALO, w), lambda i, j: (i, jnp.maximum(j * hb - 1, 0), 0))
    nxt = lambda w: pl.BlockSpec((1, HALO, w), lambda i, j: (i, jnp.minimum((j + 1) * hb, nhb - 1), 0))
    return pl.pallas_call(
        functools.partial(_local_kernel, t_len=t_len),
        grid=grid,
        in_specs=[main(SC_END), prev(SC_END), nxt(SC_END),
                  main(2 * GROUP_W), prev(2 * GROUP_W), nxt(2 * GROUP_W),
                  _const_spec((SC_KERNEL, GROUP_W)), _const_spec((CF_KERNEL, GROUP_W)),
                  _const_spec((1, GROUP_W)), _const_spec((1, GROUP_W)), _const_spec((1, GROUP_W)),
                  _const_spec((GROUP_W, GROUP_W))],
        out_specs=main(2 * GROUP_W),
        out_shape=jax.ShapeDtypeStruct((b, l, 2 * GROUP_W), jnp.bfloat16),
        scratch_shapes=[pltpu.VMEM((t_len + 2 * HALO, GROUP_W), jnp.float32),
                        pltpu.VMEM((t_len + 2 * HALO, GROUP_W), jnp.float32)],
        compiler_params=_cparams(2), name="conv_mixers",
    )(usc, usc, usc, ucf, ucf, ucf, lw["w_sc"], lw["cf_w_dw"], lw["cf_b_dw"], lw["cf_ln_g"],
      lw["cf_ln_b"], lw["cf_w_pw"])


MLP_CHUNK = 1024


def _outmlp_kernel(x_ref, yloc_ref, ymla_ref, yna_ref, mod_ref, gpm_ref, gpre_ref, gpost_ref,
                   wout_ref, w1_ref, w2_ref, o_ref):
    x = x_ref[0]
    ga1 = mod_ref[0, 2:3, :]
    sh2 = mod_ref[0, 3:4, :]
    sc2 = mod_ref[0, 4:5, :]
    ga2 = mod_ref[0, 5:6, :]
    ycat = jnp.concatenate([yloc_ref[0], ymla_ref[0], yna_ref[0]], axis=-1)
    y = _dot(ycat, wout_ref[...])
    x1 = x + ga1 * (_rms(y) * gpm_ref[...])
    hm = ((_rms(x1) * gpre_ref[...]) * (1.0 + sc2) + sh2).astype(jnp.bfloat16)
    acc = jnp.zeros(x.shape, jnp.float32)
    for c in range(MLP_HIDDEN // MLP_CHUNK):
        lo = c * MLP_CHUNK
        hj = jnp.maximum(_dot(hm, w1_ref[:, lo:lo + MLP_CHUNK]), 0.0)
        acc = acc + _dot((hj * hj).astype(jnp.bfloat16), w2_ref[lo:lo + MLP_CHUNK, :])
    o_ref[0] = x1 + ga2 * (_rms(acc) * gpost_ref[...])


def _outmlp(x, yloc, ymla, yna, mod, lw, tm):
    b, l, d = x.shape
    bm = mod.shape[0]
    grid = (b, l // tm)
    tok = lambda w: pl.BlockSpec((1, tm, w), lambda i, j: (i, j, 0))
    single = lambda shape: pl.BlockSpec(shape, lambda i, j: (0,) * len(shape),
                                        pipeline_mode=pl.Buffered(1))
    return pl.pallas_call(
        _outmlp_kernel,
        grid=grid,
        in_specs=[tok(d), tok(2 * GROUP_W), tok(MLA_PAD), tok(GROUP_W),
                  pl.BlockSpec((1, N_MOD, d), (lambda i, j: (i, 0, 0)) if bm > 1 else (lambda i, j: (0, 0, 0))),
                  _const_spec((1, d)), _const_spec((1, d)), _const_spec((1, d)),
                  single(lw["w_out"].shape), single((d, MLP_HIDDEN)), single((MLP_HIDDEN, d))],
        out_specs=tok(d),
        out_shape=jax.ShapeDtypeStruct((b, l, d), jnp.float32),
        compiler_params=_cparams(2), name="outproj_mlp",
    )(x, yloc, ymla, yna, mod, lw["g_post_mix"], lw["g_pre_mlp"], lw["g_post_mlp"],
      lw["w_out"], lw["w_mlp1"], lw["w_mlp2"])


def _swap_halves_signed(w):
    q = MLA_ROPE // 4
    x1, x2, x3, x4 = (w[..., i * q:(i + 1) * q] for i in range(4))
    return jnp.concatenate([-x2, x1, -x4, x3], axis=-1)


def _prep_layer(i, w_in, w_uq, w_ukv, w_out, p):
    bf16 = jnp.bfloat16
    wi = w_in[i]
    d = wi.shape[0]
    k_rope = wi[:, MLA_Q_END + MLA_KV_RANK:MLA_KV_END]
    z32 = jnp.zeros((d, MLA_ROPE), wi.dtype)
    w_in_p = jnp.concatenate([
        wi[:, :MLA_Q_END + MLA_KV_RANK],
        k_rope, z32, _swap_halves_signed(k_rope), z32,
        wi[:, MLA_KV_END:]], axis=-1).astype(bf16)

    wq = w_uq[i].reshape(MLA_Q_RANK, N_HEADS, MLA_NOPE + MLA_ROPE)
    zq = jnp.zeros((MLA_Q_RANK, N_HEADS, HEAD_PAD - MLA_NOPE - MLA_ROPE), wq.dtype)
    zn = jnp.zeros((MLA_Q_RANK, N_HEADS, MLA_NOPE), wq.dtype)
    q_main = jnp.concatenate([wq, zq], axis=-1).reshape(MLA_Q_RANK, MLA_PAD)
    q_swap = jnp.concatenate([zn, _swap_halves_signed(wq[..., MLA_NOPE:]), zq], axis=-1)
    w_uq_p = jnp.concatenate([q_main, q_swap.reshape(MLA_Q_RANK, MLA_PAD)], axis=-1).astype(bf16)

    wkv = w_ukv[i].reshape(MLA_KV_RANK, N_HEADS, MLA_NOPE + MLA_V)
    zk = jnp.zeros((MLA_KV_RANK, N_HEADS, HEAD_PAD - MLA_NOPE), wkv.dtype)
    zv = jnp.zeros((MLA_KV_RANK, N_HEADS, HEAD_PAD - MLA_V), wkv.dtype)
    k_main = jnp.concatenate([wkv[..., :MLA_NOPE], zk], axis=-1).reshape(MLA_KV_RANK, MLA_PAD)
    v_main = jnp.concatenate([wkv[..., MLA_NOPE:], zv], axis=-1).reshape(MLA_KV_RANK, MLA_PAD)
    w_ukv_p = jnp.concatenate([k_main, v_main], axis=-1).astype(bf16)

    wo = w_out[i]
    wo_mla = wo[GROUP_W:2 * GROUP_W].reshape(N_HEADS, MLA_V, d)
    wo_mla = jnp.concatenate([wo_mla, jnp.zeros((N_HEADS, HEAD_PAD - MLA_V, d), wo.dtype)], axis=1)
    w_out_p = jnp.concatenate([wo[:GROUP_W], wo[2 * GROUP_W:3 * GROUP_W],
                               wo_mla.reshape(MLA_PAD, d), wo[3 * GROUP_W:]], axis=0).astype(bf16)

    row = lambda a: a[i].reshape(1, -1)
    return {
        "g_pre_mix": row(p["g_pre_mix"]), "w_in": w_in_p, "g_q": row(p["g_q"]), "w_uq": w_uq_p,
        "g_kv": row(p["g_kv"]), "w_ukv": w_ukv_p, "w_sc": p["w_sc"][i], "cf_w_dw": p["cf_w_dw"][i],
        "cf_b_dw": row(p["cf_b_dw"]), "cf_ln_g": row(p["cf_ln_g"]), "cf_ln_b": row(p["cf_ln_b"]),
        "cf_w_pw": p["cf_w_pw"][i].astype(bf16), "w_out": w_out_p, "g_post_mix": row(p["g_post_mix"]),
        "g_pre_mlp": row(p["g_pre_mlp"]), "w_mlp1": p["w_mlp1"][i].astype(bf16),
        "w_mlp2": p["w_mlp2"][i].astype(bf16), "g_post_mlp": row(p["g_post_mlp"]),
        "na_bias": _na_bias(p["na_rpb"][i]),
    }


def _rope_tables(l, rotate):
    f32 = jnp.float32
    ones = jnp.ones((l, MLA_NOPE), f32)
    zpad = jnp.zeros((l, HEAD_PAD - MLA_NOPE - MLA_ROPE), f32)
    z32 = jnp.zeros((l, MLA_ROPE), f32)
    if rotate:
        pos = jnp.arange(l)
        n_freq = MLA_ROPE // 4
        inv_freq = ROPE_BASE ** (-jnp.arange(n_freq, dtype=f32) / n_freq)
        ang_row = (pos // GRID_W).astype(f32)[:, None] * inv_freq
        ang_col = (pos % GRID_W).astype(f32)[:, None] * inv_freq
        cos = jnp.concatenate([jnp.cos(ang_row)] * 2 + [jnp.cos(ang_col)] * 2, axis=-1)
        sin = jnp.concatenate([jnp.sin(ang_row)] * 2 + [jnp.sin(ang_col)] * 2, axis=-1)
    else:
        cos = jnp.ones((l, MLA_ROPE), f32)
        sin = z32
    return {"cq": jnp.concatenate([ones, cos, zpad], axis=-1),
            "sq": jnp.concatenate([jnp.zeros_like(ones), sin, zpad], axis=-1),
            "tk": jnp.concatenate([cos, z32, sin, z32], axis=-1)}


def kernel(x, c, ctx, c_ctx, w_mod, b_mod, g_pre_mix, w_in, w_sc, g_q, w_uq, g_kv, w_ukv, cf_w_dw,
           cf_b_dw, cf_ln_g, cf_ln_b, cf_w_pw, na_rpb, w_out, g_post_mix, g_pre_mlp, w_mlp1, w_mlp2,
           g_post_mlp):
    b, l, d = x.shape
    lc = ctx.shape[1]
    depth = w_mod.shape[0]
    p = {"g_pre_mix": g_pre_mix, "w_sc": w_sc, "g_q": g_q, "g_kv": g_kv, "cf_w_dw": cf_w_dw,
         "cf_b_dw": cf_b_dw, "cf_ln_g": cf_ln_g, "cf_ln_b": cf_ln_b, "cf_w_pw": cf_w_pw,
         "na_rpb": na_rpb, "g_post_mix": g_post_mix, "g_pre_mlp": g_pre_mlp, "w_mlp1": w_mlp1,
         "w_mlp2": w_mlp2, "g_post_mlp": g_post_mlp}

    cs = jnp.concatenate([c, c_ctx[None, :], jnp.zeros((8 - b - 1, d), c.dtype)], axis=0)
    mods = _modulation(cs, w_mod, b_mod).reshape(depth, 8, N_MOD, d)
    tabs_lat = _rope_tables(l, True)
    tabs_ctx = _rope_tables(lc, False)

    tm = 512
    xc = ctx
    for i in range(depth):
        last = i == depth - 1
        lw = _prep_layer(i, w_in, w_uq, w_ukv, w_out, p)
        mod = mods[i, :b]
        mod_c = mods[i, b:b + 1]

        usc, q, k, v, ucf, nq, nk, nv = _inproj(x, mod, lw, tabs_lat, tm)
        usc_c, q_c, k_c, v_c, ucf_c, nq_c, nk_c, nv_c = _inproj(xc, mod_c, lw, tabs_ctx, lc)

        y_loc = _local(usc, ucf, lw, tm)
        y_mla = _mla(q, k, v, k_c, v_c, tm)
        y_na = _na(nq, nk, nv, nk_c, nv_c, lw["na_bias"])
        x = _outmlp(x, y_loc, y_mla, y_na, mod, lw, tm)

        if not last:
            yc_loc = _local(usc_c, ucf_c, lw, lc)
            yc_mla = _mla(q_c, None, None, k_c, v_c, lc)
            yc_na = _na(nq_c, None, None, nk_c, nv_c, None)
            xc = _outmlp(xc, yc_loc, yc_mla, yc_na, mod_c, lw, lc)
    return x
```

```python
import functools

import jax
import jax.numpy as jnp
import numpy as np
from jax import lax
from jax.experimental import pallas as pl
from jax.experimental.pallas import tpu as pltpu

D_MODEL = 1024
GRID_W = 64
GROUP_W = D_MODEL // 4
HEAD_DIM = 64
N_HEADS = GROUP_W // HEAD_DIM
SC_KERNEL = 3
MLA_Q_RANK = 256
MLA_KV_RANK = 128
MLA_NOPE = 64
MLA_ROPE = 32
MLA_V = 64
MLA_SCALE = (MLA_NOPE + MLA_ROPE) ** -0.5
CF_KERNEL = 31
NA_ROWS = 8
NA_COLS = 16
NA_SCALE = HEAD_DIM ** -0.5
ROPE_BASE = 10000.0
MLP_HIDDEN = 4 * D_MODEL
N_MOD = 6
EPS = 1e-6

SC_END = 3 * GROUP_W
MLA_Q_END = SC_END + MLA_Q_RANK
MLA_KV_END = MLA_Q_END + MLA_KV_RANK + MLA_ROPE
CF_END = MLA_KV_END + 2 * GROUP_W
NA_Q_END = CF_END + GROUP_W
P_IN = NA_Q_END + 2 * GROUP_W

LANES = 128
HEAD_PAD = LANES
MLA_PAD = N_HEADS * HEAD_PAD
PC_SC = 0
PC_CQ = PC_SC + SC_END
PC_CKV = PC_CQ + MLA_Q_RANK
PC_ROPE = PC_CKV + MLA_KV_RANK
PC_CF = PC_ROPE + LANES
PC_NA = PC_CF + 2 * GROUP_W
P_PAD = PC_NA + 3 * GROUP_W

NA_QROWS = 4
NA_WROWS = NA_QROWS + NA_ROWS
NA_TQ = NA_QROWS * GRID_W
NA_TK = NA_WROWS * GRID_W
HALO = 16
MASK_VALUE = -1e30
MLA_KCHUNK = 2048
LOG2E = 1.4426950408889634

VMEM_LIMIT = 56 * 1024 * 1024


def _cparams(n_axes):
    return pltpu.CompilerParams(dimension_semantics=("arbitrary",) * n_axes,
                                vmem_limit_bytes=VMEM_LIMIT)


def _const_spec(shape):
    nd = len(shape)
    return pl.BlockSpec(shape, lambda *_: (0,) * nd)


def _rms(x):
    return x * lax.rsqrt(jnp.mean(x * x, axis=-1, keepdims=True) + EPS)


def _dot(a, b):
    return jnp.dot(a, b, preferred_element_type=jnp.float32)


def _dot_nt(a, b):
    return lax.dot_general(a, b, (((1,), (1,)), ((), ())), preferred_element_type=jnp.float32)


def _mod_kernel(cs_ref, w_ref, b_ref, o_ref):
    cs = cs_ref[...]
    a = (cs * jax.nn.sigmoid(cs)).astype(jnp.bfloat16)
    o_ref[0] = _dot(a, w_ref[0].astype(jnp.bfloat16)) + b_ref[0]


def _modulation(cs, w_mod, b_mod):
    depth, d, n = w_mod.shape
    bn = 1024
    return pl.pallas_call(
        _mod_kernel,
        grid=(depth, n // bn),
        in_specs=[pl.BlockSpec(cs.shape, lambda l, j: (0, 0)),
                  pl.BlockSpec((1, d, bn), lambda l, j: (l, 0, j)),
                  pl.BlockSpec((1, 1, bn), lambda l, j: (l, 0, j))],
        out_specs=pl.BlockSpec((1, cs.shape[0], bn), lambda l, j: (l, 0, j)),
        out_shape=jax.ShapeDtypeStruct((depth, cs.shape[0], n), jnp.float32),
        compiler_params=_cparams(2), name="modulation",
    )(cs, w_mod, b_mod.reshape(depth, 1, n))


def _inproj_kernel(x_ref, mod_ref, gpre_ref, win_ref, gq_ref, wuq_ref, gkv_ref, wukv_ref,
                   cq_ref, sq_ref, tk_ref,
                   usc_ref, q_ref, k_ref, v_ref, ucf_ref, nq_ref, nk_ref, nv_ref):
    x = x_ref[0]
    sh1 = mod_ref[0, 0:1, :]
    sc1 = mod_ref[0, 1:2, :]
    h = (_rms(x) * gpre_ref[...]) * (1.0 + sc1) + sh1
    u = _dot(h.astype(jnp.bfloat16), win_ref[...])

    usc_ref[0] = u[:, PC_SC:PC_CQ]
    ucf_ref[0] = u[:, PC_CF:PC_NA]
    nq_ref[0] = u[:, PC_NA:PC_NA + GROUP_W].astype(jnp.bfloat16)
    nk_ref[0] = u[:, PC_NA + GROUP_W:PC_NA + 2 * GROUP_W].astype(jnp.bfloat16)
    nv_ref[0] = u[:, PC_NA + 2 * GROUP_W:P_PAD].astype(jnp.bfloat16)

    cqn = (_rms(u[:, PC_CQ:PC_CKV]) * gq_ref[...]).astype(jnp.bfloat16)
    qf = _dot(cqn, wuq_ref[...])
    cq_t = cq_ref[...]
    sq_t = sq_ref[...]
    for hd in range(N_HEADS):
        lo = hd * HEAD_PAD
        qh = qf[:, lo:lo + HEAD_PAD] * cq_t + qf[:, MLA_PAD + lo:MLA_PAD + lo + HEAD_PAD] * sq_t
        q_ref[0, :, lo:lo + HEAD_PAD] = qh.astype(jnp.bfloat16)

    ckvn = (_rms(u[:, PC_CKV:PC_ROPE]) * gkv_ref[...]).astype(jnp.bfloat16)
    kvf = _dot(ckvn, wukv_ref[...])
    g = u[:, PC_ROPE:PC_CF] * tk_ref[...]
    lane = lax.broadcasted_iota(jnp.int32, g.shape, 1)
    rot = pltpu.roll(g, 2 * MLA_ROPE, axis=1) + g
    kr = jnp.where((lane >= MLA_NOPE) & (lane < MLA_NOPE + MLA_ROPE), rot, 0.0)
    for hd in range(N_HEADS):
        lo = hd * HEAD_PAD
        k_ref[0, :, lo:lo + HEAD_PAD] = (kvf[:, lo:lo + HEAD_PAD] + kr).astype(jnp.bfloat16)
        vh = kvf[:, MLA_PAD + lo:MLA_PAD + lo + HEAD_PAD]
        v_ref[0, :, lo:lo + HEAD_PAD] = jnp.where(lane == MLA_V, 1.0, vh).astype(jnp.bfloat16)


def _inproj(x, mod, lw, tabs, tm):
    b, l, d = x.shape
    bm = mod.shape[0]
    grid = (b, l // tm)
    tok = lambda w: pl.BlockSpec((1, tm, w), lambda i, j: (i, j, 0))
    tab = pl.BlockSpec((tm, LANES), lambda i, j: (j, 0))
    f32, bf16 = jnp.float32, jnp.bfloat16
    out_shape = [jax.ShapeDtypeStruct((b, l, SC_END), f32),
                 jax.ShapeDtypeStruct((b, l, MLA_PAD), bf16),
                 jax.ShapeDtypeStruct((b, l, MLA_PAD), bf16),
                 jax.ShapeDtypeStruct((b, l, MLA_PAD), bf16),
                 jax.ShapeDtypeStruct((b, l, 2 * GROUP_W), f32),
                 jax.ShapeDtypeStruct((b, l, GROUP_W), bf16),
                 jax.ShapeDtypeStruct((b, l, GROUP_W), bf16),
                 jax.ShapeDtypeStruct((b, l, GROUP_W), bf16)]
    return pl.pallas_call(
        _inproj_kernel,
        grid=grid,
        in_specs=[tok(d),
                  pl.BlockSpec((1, N_MOD, d), (lambda i, j: (i, 0, 0)) if bm > 1 else (lambda i, j: (0, 0, 0))),
                  _const_spec((1, d)),
                  _const_spec((d, P_PAD)),
                  _const_spec((1, MLA_Q_RANK)),
                  _const_spec((MLA_Q_RANK, 2 * MLA_PAD)),
                  _const_spec((1, MLA_KV_RANK)),
                  _const_spec((MLA_KV_RANK, 2 * MLA_PAD)),
                  tab, tab, tab],
        out_specs=[tok(SC_END), tok(MLA_PAD), tok(MLA_PAD), tok(MLA_PAD), tok(2 * GROUP_W),
                   tok(GROUP_W), tok(GROUP_W), tok(GROUP_W)],
        out_shape=out_shape,
        compiler_params=_cparams(2), name="inproj",
    )(x, mod, lw["g_pre_mix"], lw["w_in"], lw["g_q"], lw["w_uq"], lw["g_kv"], lw["w_ukv"],
      tabs["cq"], tabs["sq"], tabs["tk"])


def _mla_kernel(*refs, has_lat):
    if has_lat:
        q_ref, kl_ref, vl_ref, kc_ref, vc_ref, o_ref = refs
    else:
        q_ref, kc_ref, vc_ref, o_ref = refs
    q = q_ref[0]
    chunks = [(kc_ref, vc_ref, 0, kc_ref.shape[1])]
    if has_lat:
        n_lat = kl_ref.shape[1]
        chunks += [(kl_ref, vl_ref, lo, MLA_KCHUNK) for lo in range(0, n_lat, MLA_KCHUNK)]
    c2 = MLA_SCALE * LOG2E
    m = o = None
    for k_ref, v_ref, lo, n in chunks:
        s = _dot_nt(q, k_ref[0, lo:lo + n, :])
        m_chunk = jnp.max(s, axis=-1, keepdims=True)
        m_new = m_chunk if m is None else jnp.maximum(m, m_chunk)
        p = jnp.exp2((s - m_new) * c2).astype(jnp.bfloat16)
        pv = _dot(p, v_ref[0, lo:lo + n, :])
        o = pv if o is None else o * jnp.exp2((m - m_new) * c2) + pv
        m = m_new
    o_ref[0] = (o * (1.0 / o[:, MLA_V:MLA_V + 1])).astype(jnp.bfloat16)


def _mla(q, k_lat, v_lat, k_ctx, v_ctx, tq):
    b, lq, _ = q.shape
    lc = k_ctx.shape[1]
    has_lat = k_lat is not None
    grid = (b, N_HEADS, lq // tq)
    qspec = pl.BlockSpec((1, tq, HEAD_PAD), lambda i, h, j: (i, j, h))
    kvspec = lambda n: pl.BlockSpec((1, n, HEAD_PAD), lambda i, h, j: (i, 0, h))
    in_specs = [qspec]
    args = [q]
    if has_lat:
        in_specs += [kvspec(k_lat.shape[1])] * 2
        args += [k_lat, v_lat]
    in_specs += [kvspec(lc)] * 2
    args += [k_ctx, v_ctx]
    return pl.pallas_call(
        functools.partial(_mla_kernel, has_lat=has_lat),
        grid=grid,
        in_specs=in_specs,
        out_specs=qspec,
        out_shape=jax.ShapeDtypeStruct((b, lq, MLA_PAD), jnp.bfloat16),
        compiler_params=_cparams(3), name="mla_attn",
    )(*args)


def _na_kernel(*refs, has_local):
    if has_local:
        q_ref, k_ref, v_ref, kc_ref, vc_ref, bias_ref, o_ref = refs
        blk = pl.program_id(1)
        nblk = pl.num_programs(1)
        wrow = jnp.clip(blk * NA_QROWS - NA_ROWS // 2, 0, GRID_W - NA_WROWS)
        start = pl.multiple_of(wrow * GRID_W, GRID_W)
        case = jnp.where(blk == 0, 0, jnp.where(blk == nblk - 1, 2, 1))
        kwin = k_ref[0, pl.ds(start, NA_TK), :]
        vwin = v_ref[0, pl.ds(start, NA_TK), :]
    else:
        q_ref, kc_ref, vc_ref, o_ref = refs
    q = q_ref[0]
    kc = kc_ref[0]
    vc = vc_ref[0]
    lane = lax.broadcasted_iota(jnp.int32, q.shape, 1)
    out = jnp.zeros(q.shape, jnp.float32)
    for hd in range(N_HEADS):
        in_head = (lane >= hd * HEAD_DIM) & (lane < (hd + 1) * HEAD_DIM)
        qm = jnp.where(in_head, q, jnp.zeros_like(q))
        s_c = _dot_nt(qm, kc) * NA_SCALE
        m = jnp.max(s_c, axis=-1, keepdims=True)
        if has_local:
            s_l = _dot_nt(qm, kwin) * NA_SCALE + bias_ref[case, hd]
            m = jnp.maximum(m, jnp.max(s_l, axis=-1, keepdims=True))
        p_c = jnp.exp(s_c - m)
        den = jnp.sum(p_c, axis=-1, keepdims=True)
        o = _dot(p_c.astype(jnp.bfloat16), vc)
        if has_local:
            p_l = jnp.exp(s_l - m)
            den = den + jnp.sum(p_l, axis=-1, keepdims=True)
            o = o + _dot(p_l.astype(jnp.bfloat16), vwin)
        out = jnp.where(in_head, o * (1.0 / den), out)
    o_ref[0] = out.astype(jnp.bfloat16)


def _na(q, k, v, k_ctx, v_ctx, bias):
    b, l, w = q.shape
    lc = k_ctx.shape[1]
    has_local = k is not None
    tq = NA_TQ
    grid = (b, l // tq)
    qspec = pl.BlockSpec((1, tq, w), lambda i, j: (i, j, 0))
    full = lambda n: pl.BlockSpec((1, n, w), lambda i, j: (i, 0, 0))
    in_specs = [qspec]
    args = [q]
    if has_local:
        in_specs += [full(l), full(l)]
        args += [k, v]
    in_specs += [full(lc), full(lc)]
    args += [k_ctx, v_ctx]
    if has_local:
        in_specs += [_const_spec(bias.shape)]
        args += [bias]
    return pl.pallas_call(
        functools.partial(_na_kernel, has_local=has_local),
        grid=grid,
        in_specs=in_specs,
        out_specs=qspec,
        out_shape=jax.ShapeDtypeStruct((b, l, w), jnp.bfloat16),
        compiler_params=_cparams(2), name="na_attn",
    )(*args)


def _na_bias(rpb):
    rows = w = GRID_W
    n_off_r = 2 * NA_ROWS - 1
    delta = np.arange(2 * w - 1) - (w - 1)
    by_delta = rpb[:, :, np.clip(delta + (NA_COLS - 1), 0, 2 * NA_COLS - 2)]
    by_delta = jnp.pad(by_delta, ((0, 0), (0, 0), (0, 1)))
    tiled = jnp.broadcast_to(by_delta[:, :, None, :], (N_HEADS, n_off_r, w, 2 * w))
    tc = tiled.reshape(N_HEADS, n_off_r, 2 * w * w)[:, :, :w * (2 * w - 1)]
    tc = tc.reshape(N_HEADS, n_off_r, w, 2 * w - 1)[..., w - 1:]
    col = np.arange(w)
    cstart = np.clip(col - NA_COLS // 2, 0, w - NA_COLS)
    col_ok = (col[None, :] >= cstart[:, None]) & (col[None, :] < cstart[:, None] + NA_COLS)
    tc = jnp.where(col_ok, tc, MASK_VALUE)
    masked = jnp.full((N_HEADS, w, w), MASK_VALUE, tc.dtype)
    cases = []
    for r0 in (0, NA_QROWS, rows - NA_QROWS):
        wrow = min(max(r0 - NA_ROWS // 2, 0), rows - NA_WROWS)
        qrows = []
        for qi in range(NA_QROWS):
            r = r0 + qi
            band = min(max(r - NA_ROWS // 2, 0), rows - NA_ROWS)
            blocks = []
            for kj in range(NA_WROWS):
                kr = wrow + kj
                in_band = band <= kr < band + NA_ROWS
                blocks.append(tc[:, kr - r + (NA_ROWS - 1)] if in_band else masked)
            qrows.append(jnp.concatenate(blocks, axis=-1))
        cases.append(jnp.concatenate(qrows, axis=-2))
    return jnp.stack(cases, axis=0).astype(jnp.float32)


def _local_kernel(usc_ref, uscp_ref, uscn_ref, ucf_ref, ucfp_ref, ucfn_ref,
                  wsc_ref, wdw_ref, bdw_ref, lng_ref, lnb_ref, wpw_ref,
                  o_ref, ext_sc, ext_cf, *, t_len):
    j = pl.program_id(1)
    prev_ok = j > 0
    next_ok = j < pl.num_programs(1) - 1

    def gated_sc(u):
        return u[:, 2 * GROUP_W:3 * GROUP_W] * u[:, 0:GROUP_W]

    def glu(u):
        return u[:, 0:GROUP_W] * jax.nn.sigmoid(u[:, GROUP_W:2 * GROUP_W])

    usc = usc_ref[0]
    ext_sc[0:HALO, :] = jnp.where(prev_ok, gated_sc(uscp_ref[0]), 0.0)
    ext_sc[HALO:HALO + t_len, :] = gated_sc(usc)
    ext_sc[HALO + t_len:2 * HALO + t_len, :] = jnp.where(next_ok, gated_sc(uscn_ref[0]), 0.0)
    ext_cf[0:HALO, :] = jnp.where(prev_ok, glu(ucfp_ref[0]), 0.0)
    ext_cf[HALO:HALO + t_len, :] = glu(ucf_ref[0])
    ext_cf[HALO + t_len:2 * HALO + t_len, :] = jnp.where(next_ok, glu(ucfn_ref[0]), 0.0)

    acc = jnp.zeros((t_len, GROUP_W), jnp.float32)
    for k in range(SC_KERNEL):
        off = HALO - SC_KERNEL // 2 + k
        acc = acc + ext_sc[off:off + t_len, :] * wsc_ref[k:k + 1, :]
    y_sc = usc[:, GROUP_W:2 * GROUP_W] * acc

    acc = jnp.zeros((t_len, GROUP_W), jnp.float32)
    for k in range(CF_KERNEL):
        off = HALO - CF_KERNEL // 2 + k
        acc = acc + ext_cf[off:off + t_len, :] * wdw_ref[k:k + 1, :]
    y = acc + bdw_ref[...]
    mu = jnp.mean(y, axis=-1, keepdims=True)
    yc = y - mu
    var = jnp.mean(yc * yc, axis=-1, keepdims=True)
    z = (yc * lax.rsqrt(var + EPS)) * lng_ref[...] + lnb_ref[...]
    z = z * jax.nn.sigmoid(z)
    y_cf = _dot(z.astype(jnp.bfloat16), wpw_ref[...])

    o_ref[0, :, 0:GROUP_W] = y_sc.astype(jnp.bfloat16)
    o_ref[0, :, GROUP_W:2 * GROUP_W] = y_cf.astype(jnp.bfloat16)


def _local(usc, ucf, lw, t_len):
    b, l, _ = usc.shape
    nt = l // t_len
    hb = t_len // HALO
    nhb = l // HALO
    grid = (b, nt)
    main = lambda w: pl.BlockSpec((1, t_len, w), lambda i, j: (i, j, 0))
    prev = lambda w: pl.BlockSpec((1, HALO, w), lambda i, j: (i, jnp.maximum(j * hb - 1, 0), 0))
    nxt = lambda w: pl.BlockSpec((1, HALO, w), lambda i, j: (i, jnp.minimum((j + 1) * hb, nhb - 1), 0))
    return pl.pallas_call(
        functools.partial(_local_kernel, t_len=t_len),
        grid=grid,
        in_specs=[main(SC_END), prev(SC_END), nxt(SC_END),
                  main(2 * GROUP_W), prev(2 * GROUP_W), nxt(2 * GROUP_W),
                  _const_spec((SC_KERNEL, GROUP_W)), _const_spec((CF_KERNEL, GROUP_W)),
                  _const_spec((1, GROUP_W)), _const_spec((1, GROUP_W)), _const_spec((1, GROUP_W)),
                  _const_spec((GROUP_W, GROUP_W))],
        out_specs=main(2 * GROUP_W),
        out_shape=jax.ShapeDtypeStruct((b, l, 2 * GROUP_W), jnp.bfloat16),
        scratch_shapes=[pltpu.VMEM((t_len + 2 * HALO, GROUP_W), jnp.float32),
                        pltpu.VMEM((t_len + 2 * HALO, GROUP_W), jnp.float32)],
        compiler_params=_cparams(2), name="conv_mixers",
    )(usc, usc, usc, ucf, ucf, ucf, lw["w_sc"], lw["cf_w_dw"], lw["cf_b_dw"], lw["cf_ln_g"],
      lw["cf_ln_b"], lw["cf_w_pw"])


MLP_CHUNK = 1024


def _outmlp_kernel(x_ref, yloc_ref, ymla_ref, yna_ref, mod_ref, gpm_ref, gpre_ref, gpost_ref,
                   wout_ref, w1_ref, w2_ref, o_ref):
    x = x_ref[0]
    ga1 = mod_ref[0, 2:3, :]
    sh2 = mod_ref[0, 3:4, :]
    sc2 = mod_ref[0, 4:5, :]
    ga2 = mod_ref[0, 5:6, :]
    ycat = jnp.concatenate([yloc_ref[0], ymla_ref[0], yna_ref[0]], axis=-1)
    y = _dot(ycat, wout_ref[...])
    x1 = x + ga1 * (_rms(y) * gpm_ref[...])
    hm = ((_rms(x1) * gpre_ref[...]) * (1.0 + sc2) + sh2).astype(jnp.bfloat16)
    acc = jnp.zeros(x.shape, jnp.float32)
    for c in range(MLP_HIDDEN // MLP_CHUNK):
        lo = c * MLP_CHUNK
        hj = jnp.maximum(_dot(hm, w1_ref[:, lo:lo + MLP_CHUNK]), 0.0)
        acc = acc + _dot((hj * hj).astype(jnp.bfloat16), w2_ref[lo:lo + MLP_CHUNK, :])
    o_ref[0] = x1 + ga2 * (_rms(acc) * gpost_ref[...])


def _outmlp(x, yloc, ymla, yna, mod, lw, tm):
    b, l, d = x.shape
    bm = mod.shape[0]
    grid = (b, l // tm)
    tok = lambda w: pl.BlockSpec((1, tm, w), lambda i, j: (i, j, 0))
    single = lambda shape: pl.BlockSpec(shape, lambda i, j: (0,) * len(shape),
                                        pipeline_mode=pl.Buffered(1))
    return pl.pallas_call(
        _outmlp_kernel,
        grid=grid,
        in_specs=[tok(d), tok(2 * GROUP_W), tok(MLA_PAD), tok(GROUP_W),
                  pl.BlockSpec((1, N_MOD, d), (lambda i, j: (i, 0, 0)) if bm > 1 else (lambda i, j: (0, 0, 0))),
                  _const_spec((1, d)), _const_spec((1, d)), _const_spec((1, d)),
                  single(lw["w_out"].shape), single((d, MLP_HIDDEN)), single((MLP_HIDDEN, d))],
        out_specs=tok(d),
        out_shape=jax.ShapeDtypeStruct((b, l, d), jnp.float32),
        compiler_params=_cparams(2), name="outproj_mlp",
    )(x, yloc, ymla, yna, mod, lw["g_post_mix"], lw["g_pre_mlp"], lw["g_post_mlp"],
      lw["w_out"], lw["w_mlp1"], lw["w_mlp2"])


def _swap_halves_signed(w):
    q = MLA_ROPE // 4
    x1, x2, x3, x4 = (w[..., i * q:(i + 1) * q] for i in range(4))
    return jnp.concatenate([-x2, x1, -x4, x3], axis=-1)


def _prep_layer(i, w_in, w_uq, w_ukv, w_out, p):
    bf16 = jnp.bfloat16
    wi = w_in[i]
    d = wi.shape[0]
    k_rope = wi[:, MLA_Q_END + MLA_KV_RANK:MLA_KV_END]
    z32 = jnp.zeros((d, MLA_ROPE), wi.dtype)
    w_in_p = jnp.concatenate([
        wi[:, :MLA_Q_END + MLA_KV_RANK],
        k_rope, z32, _swap_halves_signed(k_rope), z32,
        wi[:, MLA_KV_END:]], axis=-1).astype(bf16)

    wq = w_uq[i].reshape(MLA_Q_RANK, N_HEADS, MLA_NOPE + MLA_ROPE)
    zq = jnp.zeros((MLA_Q_RANK, N_HEADS, HEAD_PAD - MLA_NOPE - MLA_ROPE), wq.dtype)
    zn = jnp.zeros((MLA_Q_RANK, N_HEADS, MLA_NOPE), wq.dtype)
    q_main = jnp.concatenate([wq, zq], axis=-1).reshape(MLA_Q_RANK, MLA_PAD)
    q_swap = jnp.concatenate([zn, _swap_halves_signed(wq[..., MLA_NOPE:]), zq], axis=-1)
    w_uq_p = jnp.concatenate([q_main, q_swap.reshape(MLA_Q_RANK, MLA_PAD)], axis=-1).astype(bf16)

    wkv = w_ukv[i].reshape(MLA_KV_RANK, N_HEADS, MLA_NOPE + MLA_V)
    zk = jnp.zeros((MLA_KV_RANK, N_HEADS, HEAD_PAD - MLA_NOPE), wkv.dtype)
    zv = jnp.zeros((MLA_KV_RANK, N_HEADS, HEAD_PAD - MLA_V), wkv.dtype)
    k_main = jnp.concatenate([wkv[..., :MLA_NOPE], zk], axis=-1).reshape(MLA_KV_RANK, MLA_PAD)
    v_main = jnp.concatenate([wkv[..., MLA_NOPE:], zv], axis=-1).reshape(MLA_KV_RANK, MLA_PAD)
    w_ukv_p = jnp.concatenate([k_main, v_main], axis=-1).astype(bf16)

    wo = w_out[i]
    wo_mla = wo[GROUP_W:2 * GROUP_W].reshape(N_HEADS, MLA_V, d)
    wo_mla = jnp.concatenate([wo_mla, jnp.zeros((N_HEADS, HEAD_PAD - MLA_V, d), wo.dtype)], axis=1)
    w_out_p = jnp.concatenate([wo[:GROUP_W], wo[2 * GROUP_W:3 * GROUP_W],
                               wo_mla.reshape(MLA_PAD, d), wo[3 * GROUP_W:]], axis=0).astype(bf16)

    row = lambda a: a[i].reshape(1, -1)
    return {
        "g_pre_mix": row(p["g_pre_mix"]), "w_in": w_in_p, "g_q": row(p["g_q"]), "w_uq": w_uq_p,
        "g_kv": row(p["g_kv"]), "w_ukv": w_ukv_p, "w_sc": p["w_sc"][i], "cf_w_dw": p["cf_w_dw"][i],
        "cf_b_dw": row(p["cf_b_dw"]), "cf_ln_g": row(p["cf_ln_g"]), "cf_ln_b": row(p["cf_ln_b"]),
        "cf_w_pw": p["cf_w_pw"][i].astype(bf16), "w_out": w_out_p, "g_post_mix": row(p["g_post_mix"]),
        "g_pre_mlp": row(p["g_pre_mlp"]), "w_mlp1": p["w_mlp1"][i].astype(bf16),
        "w_mlp2": p["w_mlp2"][i].astype(bf16), "g_post_mlp": row(p["g_post_mlp"]),
        "na_bias": _na_bias(p["na_rpb"][i]),
    }


def _rope_tables(l, rotate):
    f32 = jnp.float32
    ones = jnp.ones((l, MLA_NOPE), f32)
    zpad = jnp.zeros((l, HEAD_PAD - MLA_NOPE - MLA_ROPE), f32)
    z32 = jnp.zeros((l, MLA_ROPE), f32)
    if rotate:
        pos = jnp.arange(l)
        n_freq = MLA_ROPE // 4
        inv_freq = ROPE_BASE ** (-jnp.arange(n_freq, dtype=f32) / n_freq)
        ang_row = (pos // GRID_W).astype(f32)[:, None] * inv_freq
        ang_col = (pos % GRID_W).astype(f32)[:, None] * inv_freq
        cos = jnp.concatenate([jnp.cos(ang_row)] * 2 + [jnp.cos(ang_col)] * 2, axis=-1)
        sin = jnp.concatenate([jnp.sin(ang_row)] * 2 + [jnp.sin(ang_col)] * 2, axis=-1)
    else:
        cos = jnp.ones((l, MLA_ROPE), f32)
        sin = z32
    return {"cq": jnp.concatenate([ones, cos, zpad], axis=-1),
            "sq": jnp.concatenate([jnp.zeros_like(ones), sin, zpad], axis=-1),
            "tk": jnp.concatenate([cos, z32, sin, z32], axis=-1)}


def kernel(x, c, ctx, c_ctx, w_mod, b_mod, g_pre_mix, w_in, w_sc, g_q, w_uq, g_kv, w_ukv, cf_w_dw,
           cf_b_dw, cf_ln_g, cf_ln_b, cf_w_pw, na_rpb, w_out, g_post_mix, g_pre_mlp, w_mlp1, w_mlp2,
           g_post_mlp):
    b, l, d = x.shape
    lc = ctx.shape[1]
    depth = w_mod.shape[0]
    p = {"g_pre_mix": g_pre_mix, "w_sc": w_sc, "g_q": g_q, "g_kv": g_kv, "cf_w_dw": cf_w_dw,
         "cf_b_dw": cf_b_dw, "cf_ln_g": cf_ln_g, "cf_ln_b": cf_ln_b, "cf_w_pw": cf_w_pw,
         "na_rpb": na_rpb, "g_post_mix": g_post_mix, "g_pre_mlp": g_pre_mlp, "w_mlp1": w_mlp1,
         "w_mlp2": w_mlp2, "g_post_mlp": g_post_mlp}

    cs = jnp.concatenate([c, c_ctx[None, :], jnp.zeros((8 - b - 1, d), c.dtype)], axis=0)
    mods = _modulation(cs, w_mod, b_mod).reshape(depth, 8, N_MOD, d)
    tabs_lat = _rope_tables(l, True)
    tabs_ctx = _rope_tables(lc, False)

    tm = 512
    xc = ctx
    for i in range(depth):
        last = i == depth - 1
        lw = _prep_layer(i, w_in, w_uq, w_ukv, w_out, p)
        mod = mods[i, :b]
        mod_c = mods[i, b:b + 1]

        usc, q, k, v, ucf, nq, nk, nv = _inproj(x, mod, lw, tabs_lat, tm)
        usc_c, q_c, k_c, v_c, ucf_c, nq_c, nk_c, nv_c = _inproj(xc, mod_c, lw, tabs_ctx, lc)

        y_loc = _local(usc, ucf, lw, tm)
        y_mla = _mla(q, k, v, k_c, v_c, 1024)
        y_na = _na(nq, nk, nv, nk_c, nv_c, lw["na_bias"])
        x = _outmlp(x, y_loc, y_mla, y_na, mod, lw, tm)

        if not last:
            yc_loc = _local(usc_c, ucf_c, lw, lc)
            yc_mla = _mla(q_c, None, None, k_c, v_c, lc)
            yc_na = _na(nq_c, None, None, nk_c, nv_c, None)
            xc = _outmlp(xc, yc_loc, yc_mla, yc_na, mod_c, lw, lc)
    return x
```

```python
import functools

import jax
import jax.numpy as jnp
import numpy as np
from jax import lax
from jax.experimental import pallas as pl
from jax.experimental.pallas import tpu as pltpu

D_MODEL = 1024
GRID_W = 64
GROUP_W = D_MODEL // 4
HEAD_DIM = 64
N_HEADS = GROUP_W // HEAD_DIM
SC_KERNEL = 3
MLA_Q_RANK = 256
MLA_KV_RANK = 128
MLA_NOPE = 64
MLA_ROPE = 32
MLA_V = 64
MLA_SCALE = (MLA_NOPE + MLA_ROPE) ** -0.5
CF_KERNEL = 31
NA_ROWS = 8
NA_COLS = 16
NA_SCALE = HEAD_DIM ** -0.5
ROPE_BASE = 10000.0
MLP_HIDDEN = 4 * D_MODEL
N_MOD = 6
EPS = 1e-6

SC_END = 3 * GROUP_W
MLA_Q_END = SC_END + MLA_Q_RANK
MLA_KV_END = MLA_Q_END + MLA_KV_RANK + MLA_ROPE
CF_END = MLA_KV_END + 2 * GROUP_W
NA_Q_END = CF_END + GROUP_W
P_IN = NA_Q_END + 2 * GROUP_W

LANES = 128
SUBLANES = 8
HEAD_PAD = LANES
MLA_PAD = N_HEADS * HEAD_PAD
PC_SC = 0
PC_CQ = PC_SC + SC_END
PC_CKV = PC_CQ + MLA_Q_RANK
PC_ROPE = PC_CKV + MLA_KV_RANK
PC_CF = PC_ROPE + LANES
PC_NA = PC_CF + 2 * GROUP_W
P_PAD = PC_NA + 3 * GROUP_W

NA_QROWS = 4
NA_WROWS = NA_QROWS + NA_ROWS
NA_TQ = NA_QROWS * GRID_W
NA_TK = NA_WROWS * GRID_W
HALO = 16
CONV_ROWS = 64
MASK_VALUE = -1e30
MLA_KCHUNK = 2048
LOG2E = 1.4426950408889634
MLP_CHUNK = 1024

VMEM_LIMIT = 56 * 1024 * 1024


def _cparams(n_axes):
    return pltpu.CompilerParams(dimension_semantics=("arbitrary",) * n_axes,
                                vmem_limit_bytes=VMEM_LIMIT)


def _layer_spec(arr, layer, single_buffer=False):
    nd = arr.ndim - 1
    kwargs = {"pipeline_mode": pl.Buffered(1)} if single_buffer else {}
    return pl.BlockSpec((None,) + arr.shape[1:], lambda *_: (layer,) + (0,) * nd, **kwargs)


def _mod_spec(mods, layer, row):
    if row is None:
        return pl.BlockSpec((None, None) + mods.shape[2:], lambda i, j: (layer, i, 0, 0))
    return pl.BlockSpec((None, None) + mods.shape[2:], lambda i, j: (layer, row, 0, 0))


def _rms(x):
    return x * lax.rsqrt(jnp.mean(x * x, axis=-1, keepdims=True) + EPS)


def _dot(a, b):
    return jnp.dot(a, b, preferred_element_type=jnp.float32)


def _dot_nt(a, b):
    return lax.dot_general(a, b, (((1,), (1,)), ((), ())), preferred_element_type=jnp.float32)


def _mod_kernel(cs_ref, w_ref, b_ref, o_ref):
    cs = cs_ref[...]
    a = (cs * jax.nn.sigmoid(cs)).astype(jnp.bfloat16)
    o_ref[0] = _dot(a, w_ref[0].astype(jnp.bfloat16)) + b_ref[0]


def _modulation(cs, w_mod, b_mod):
    depth, d, n = w_mod.shape
    bn = 1024
    return pl.pallas_call(
        _mod_kernel,
        grid=(depth, n // bn),
        in_specs=[pl.BlockSpec(cs.shape, lambda l, j: (0, 0)),
                  pl.BlockSpec((1, d, bn), lambda l, j: (l, 0, j)),
                  pl.BlockSpec((1, 1, bn), lambda l, j: (l, 0, j))],
        out_specs=pl.BlockSpec((1, cs.shape[0], bn), lambda l, j: (l, 0, j)),
        out_shape=jax.ShapeDtypeStruct((depth, cs.shape[0], n), jnp.float32),
        compiler_params=_cparams(2), name="modulation",
    )(cs, w_mod, b_mod.reshape(depth, 1, n))


def _inproj_kernel(x_ref, mod_ref, gpre_ref, win_ref, gq_ref, wuq_ref, gkv_ref, wukv_ref,
                   cq_ref, sq_ref, tk_ref,
                   usc_ref, q_ref, k_ref, v_ref, ucf_ref, nq_ref, nk_ref, nv_ref):
    x = x_ref[0]
    sh1 = mod_ref[0:1, :]
    sc1 = mod_ref[1:2, :]
    h = (_rms(x) * gpre_ref[...]) * (1.0 + sc1) + sh1
    u = _dot(h.astype(jnp.bfloat16), win_ref[...])

    usc_ref[0] = u[:, PC_SC:PC_CQ]
    ucf_ref[0] = u[:, PC_CF:PC_NA]
    nq_ref[0] = u[:, PC_NA:PC_NA + GROUP_W].astype(jnp.bfloat16)
    nk_ref[0] = u[:, PC_NA + GROUP_W:PC_NA + 2 * GROUP_W].astype(jnp.bfloat16)
    nv_ref[0] = u[:, PC_NA + 2 * GROUP_W:P_PAD].astype(jnp.bfloat16)

    cqn = (_rms(u[:, PC_CQ:PC_CKV]) * gq_ref[...]).astype(jnp.bfloat16)
    qf = _dot(cqn, wuq_ref[...])
    cq_t = cq_ref[...]
    sq_t = sq_ref[...]
    for hd in range(N_HEADS):
        lo = hd * HEAD_PAD
        qh = qf[:, lo:lo + HEAD_PAD] * cq_t + qf[:, MLA_PAD + lo:MLA_PAD + lo + HEAD_PAD] * sq_t
        q_ref[0, :, lo:lo + HEAD_PAD] = qh.astype(jnp.bfloat16)

    ckvn = (_rms(u[:, PC_CKV:PC_ROPE]) * gkv_ref[...]).astype(jnp.bfloat16)
    kvf = _dot(ckvn, wukv_ref[...])
    g = u[:, PC_ROPE:PC_CF] * tk_ref[...]
    lane = lax.broadcasted_iota(jnp.int32, g.shape, 1)
    rot = pltpu.roll(g, 2 * MLA_ROPE, axis=1) + g
    kr = jnp.where((lane >= MLA_NOPE) & (lane < MLA_NOPE + MLA_ROPE), rot, 0.0)
    for hd in range(N_HEADS):
        lo = hd * HEAD_PAD
        k_ref[0, :, lo:lo + HEAD_PAD] = (kvf[:, lo:lo + HEAD_PAD] + kr).astype(jnp.bfloat16)
        vh = kvf[:, MLA_PAD + lo:MLA_PAD + lo + HEAD_PAD]
        v_ref[0, :, lo:lo + HEAD_PAD] = jnp.where(lane == MLA_V, 1.0, vh).astype(jnp.bfloat16)


def _inproj(x, mods, mod_row, wts, layer, tabs, tm):
    b, l, d = x.shape
    grid = (b, l // tm)
    tok = lambda w: pl.BlockSpec((1, tm, w), lambda i, j: (i, j, 0))
    tab = pl.BlockSpec((tm, LANES), lambda i, j: (j, 0))
    f32, bf16 = jnp.float32, jnp.bfloat16
    out_shape = [jax.ShapeDtypeStruct((b, l, SC_END), f32),
                 jax.ShapeDtypeStruct((b, l, MLA_PAD), bf16),
                 jax.ShapeDtypeStruct((b, l, MLA_PAD), bf16),
                 jax.ShapeDtypeStruct((b, l, MLA_PAD), bf16),
                 jax.ShapeDtypeStruct((b, l, 2 * GROUP_W), f32),
                 jax.ShapeDtypeStruct((b, l, GROUP_W), bf16),
                 jax.ShapeDtypeStruct((b, l, GROUP_W), bf16),
                 jax.ShapeDtypeStruct((b, l, GROUP_W), bf16)]
    names = ["g_pre_mix", "w_in", "g_q", "w_uq", "g_kv", "w_ukv"]
    return pl.pallas_call(
        _inproj_kernel,
        grid=grid,
        in_specs=[tok(d), _mod_spec(mods, layer, mod_row)]
                 + [_layer_spec(wts[n], layer) for n in names] + [tab, tab, tab],
        out_specs=[tok(SC_END), tok(MLA_PAD), tok(MLA_PAD), tok(MLA_PAD), tok(2 * GROUP_W),
                   tok(GROUP_W), tok(GROUP_W), tok(GROUP_W)],
        out_shape=out_shape,
        compiler_params=_cparams(2), name="inproj",
    )(x, mods, *[wts[n] for n in names], tabs["cq"], tabs["sq"], tabs["tk"])


def _mla_kernel(*refs, has_lat):
    if has_lat:
        q_ref, kl_ref, vl_ref, kc_ref, vc_ref, o_ref = refs
    else:
        q_ref, kc_ref, vc_ref, o_ref = refs
    q = q_ref[0]
    chunks = [(kc_ref, vc_ref, 0, kc_ref.shape[1])]
    if has_lat:
        n_lat = kl_ref.shape[1]
        chunks += [(kl_ref, vl_ref, lo, MLA_KCHUNK) for lo in range(0, n_lat, MLA_KCHUNK)]
    c2 = MLA_SCALE * LOG2E
    m = o = None
    for k_ref, v_ref, lo, n in chunks:
        s = _dot_nt(q, k_ref[0, lo:lo + n, :])
        m_chunk = jnp.max(s, axis=-1, keepdims=True)
        m_new = m_chunk if m is None else jnp.maximum(m, m_chunk)
        p = jnp.exp2((s - m_new) * c2).astype(jnp.bfloat16)
        pv = _dot(p, v_ref[0, lo:lo + n, :])
        o = pv if o is None else o * jnp.exp2((m - m_new) * c2) + pv
        m = m_new
    o_ref[0] = (o * (1.0 / o[:, MLA_V:MLA_V + 1])).astype(jnp.bfloat16)


def _mla(q, k_lat, v_lat, k_ctx, v_ctx, tq):
    b, lq, _ = q.shape
    lc = k_ctx.shape[1]
    has_lat = k_lat is not None
    grid = (b, N_HEADS, lq // tq)
    qspec = pl.BlockSpec((1, tq, HEAD_PAD), lambda i, h, j: (i, j, h))
    kvspec = lambda n: pl.BlockSpec((1, n, HEAD_PAD), lambda i, h, j: (i, 0, h))
    in_specs = [qspec]
    args = [q]
    if has_lat:
        in_specs += [kvspec(k_lat.shape[1])] * 2
        args += [k_lat, v_lat]
    in_specs += [kvspec(lc)] * 2
    args += [k_ctx, v_ctx]
    return pl.pallas_call(
        functools.partial(_mla_kernel, has_lat=has_lat),
        grid=grid,
        in_specs=in_specs,
        out_specs=qspec,
        out_shape=jax.ShapeDtypeStruct((b, lq, MLA_PAD), jnp.bfloat16),
        compiler_params=_cparams(3), name="mla_attn",
    )(*args)


def _na_kernel(*refs, has_local):
    if has_local:
        q_ref, k_ref, v_ref, kc_ref, vc_ref, bias_ref, o_ref = refs
        blk = pl.program_id(1)
        nblk = pl.num_programs(1)
        wrow = jnp.clip(blk * NA_QROWS - NA_ROWS // 2, 0, GRID_W - NA_WROWS)
        start = pl.multiple_of(wrow * GRID_W, GRID_W)
        case = jnp.where(blk == 0, 0, jnp.where(blk == nblk - 1, 2, 1))
        kwin = k_ref[0, pl.ds(start, NA_TK), :]
        vwin = v_ref[0, pl.ds(start, NA_TK), :]
    else:
        q_ref, kc_ref, vc_ref, o_ref = refs
    q = q_ref[0]
    kc = kc_ref[0]
    vc = vc_ref[0]
    lane = lax.broadcasted_iota(jnp.int32, q.shape, 1)
    out = jnp.zeros(q.shape, jnp.float32)
    c2 = NA_SCALE * LOG2E
    for hd in range(N_HEADS):
        in_head = (lane >= hd * HEAD_DIM) & (lane < (hd + 1) * HEAD_DIM)
        qm = jnp.where(in_head, q, jnp.zeros_like(q))
        s_c = _dot_nt(qm, kc) * c2
        m = jnp.max(s_c, axis=-1, keepdims=True)
        if has_local:
            s_l = _dot_nt(qm, kwin) * c2 + bias_ref[case, hd]
            m = jnp.maximum(m, jnp.max(s_l, axis=-1, keepdims=True))
        p_c = jnp.exp2(s_c - m)
        den = jnp.sum(p_c, axis=-1, keepdims=True)
        o = _dot(p_c.astype(jnp.bfloat16), vc)
        if has_local:
            p_l = jnp.exp2(s_l - m)
            den = den + jnp.sum(p_l, axis=-1, keepdims=True)
            o = o + _dot(p_l.astype(jnp.bfloat16), vwin)
        out = jnp.where(in_head, o * (1.0 / den), out)
    o_ref[0] = out.astype(jnp.bfloat16)


def _na(q, k, v, k_ctx, v_ctx, bias, layer):
    b, l, w = q.shape
    lc = k_ctx.shape[1]
    has_local = k is not None
    tq = NA_TQ
    grid = (b, l // tq)
    qspec = pl.BlockSpec((1, tq, w), lambda i, j: (i, j, 0))
    full = lambda n: pl.BlockSpec((1, n, w), lambda i, j: (i, 0, 0))
    in_specs = [qspec]
    args = [q]
    if has_local:
        in_specs += [full(l), full(l)]
        args += [k, v]
    in_specs += [full(lc), full(lc)]
    args += [k_ctx, v_ctx]
    if has_local:
        in_specs += [_layer_spec(bias, layer)]
        args += [bias]
    return pl.pallas_call(
        functools.partial(_na_kernel, has_local=has_local),
        grid=grid,
        in_specs=in_specs,
        out_specs=qspec,
        out_shape=jax.ShapeDtypeStruct((b, l, w), jnp.bfloat16),
        compiler_params=_cparams(2), name="na_attn",
    )(*args)


def _na_bias(rpb):
    depth = rpb.shape[0]
    rows = w = GRID_W
    n_off_r = 2 * NA_ROWS - 1
    delta = np.arange(2 * w - 1) - (w - 1)
    by_delta = rpb[..., np.clip(delta + (NA_COLS - 1), 0, 2 * NA_COLS - 2)]
    by_delta = jnp.pad(by_delta, ((0, 0), (0, 0), (0, 0), (0, 1)))
    tiled = jnp.broadcast_to(by_delta[..., None, :], (depth, N_HEADS, n_off_r, w, 2 * w))
    tc = tiled.reshape(depth, N_HEADS, n_off_r, 2 * w * w)[..., :w * (2 * w - 1)]
    tc = tc.reshape(depth, N_HEADS, n_off_r, w, 2 * w - 1)[..., w - 1:]
    col = np.arange(w)
    cstart = np.clip(col - NA_COLS // 2, 0, w - NA_COLS)
    col_ok = (col[None, :] >= cstart[:, None]) & (col[None, :] < cstart[:, None] + NA_COLS)
    tc = jnp.where(col_ok, tc * LOG2E, MASK_VALUE)
    masked = jnp.full((depth, N_HEADS, 1, w, w), MASK_VALUE, tc.dtype)
    planes = jnp.concatenate([tc, masked], axis=2)
    idx = np.full((3, NA_QROWS, NA_WROWS), n_off_r, np.int32)
    for c, r0 in enumerate((0, NA_QROWS, rows - NA_QROWS)):
        wrow = min(max(r0 - NA_ROWS // 2, 0), rows - NA_WROWS)
        for qi in range(NA_QROWS):
            r = r0 + qi
            band = min(max(r - NA_ROWS // 2, 0), rows - NA_ROWS)
            for kj in range(NA_WROWS):
                kr = wrow + kj
                if band <= kr < band + NA_ROWS:
                    idx[c, qi, kj] = kr - r + (NA_ROWS - 1)
    bias = jnp.take(planes, idx.reshape(-1), axis=2)
    bias = bias.reshape(depth, N_HEADS, 3, NA_QROWS, NA_WROWS, w, w)
    bias = jnp.transpose(bias, (0, 2, 1, 3, 5, 4, 6))
    return bias.reshape(depth, 3, N_HEADS, NA_TQ, NA_TK).astype(jnp.float32)


def _local_kernel(usc_ref, uscp_ref, uscn_ref, ucf_ref, ucfp_ref, ucfn_ref,
                  wsc_ref, wdw_ref, bdw_ref, lng_ref, lnb_ref, wpw_ref,
                  o_ref, ext_sc, ext_cf, shifted, conv_out, *, t_len):
    j = pl.program_id(1)
    prev_ok = j > 0
    next_ok = j < pl.num_programs(1) - 1

    def gated_sc(u):
        return u[:, 2 * GROUP_W:3 * GROUP_W] * u[:, 0:GROUP_W]

    def glu(u):
        return u[:, 0:GROUP_W] * jax.nn.sigmoid(u[:, GROUP_W:2 * GROUP_W])

    usc = usc_ref[0]
    ext_sc[0:HALO, :] = jnp.where(prev_ok, gated_sc(uscp_ref[0]), 0.0)
    ext_sc[HALO:HALO + t_len, :] = gated_sc(usc)
    ext_sc[HALO + t_len:2 * HALO + t_len, :] = jnp.where(next_ok, gated_sc(uscn_ref[0]), 0.0)
    ext_cf[0:HALO, :] = jnp.where(prev_ok, glu(ucfp_ref[0]), 0.0)
    ext_cf[HALO:HALO + t_len, :] = glu(ucf_ref[0])
    ext_cf[HALO + t_len:2 * HALO + t_len, :] = jnp.where(next_ok, glu(ucfn_ref[0]), 0.0)

    acc = jnp.zeros((t_len, GROUP_W), jnp.float32)
    for k in range(SC_KERNEL):
        off = HALO - SC_KERNEL // 2 + k
        acc = acc + ext_sc[off:off + t_len, :] * wsc_ref[k:k + 1, :]
    y_sc = usc[:, GROUP_W:2 * GROUP_W] * acc

    base = HALO - CF_KERNEL // 2
    s_len = shifted.shape[1]
    ext_all = ext_cf[...]
    n_ext = ext_all.shape[0]
    for r in range(1, SUBLANES):
        shifted[r] = pltpu.roll(ext_all, n_ext - r, axis=0)[0:s_len, :]
    for c0 in range(0, t_len, CONV_ROWS):
        acc = jnp.zeros((CONV_ROWS, GROUP_W), jnp.float32)
        for k in range(CF_KERNEL):
            r, lo = (base + k) % SUBLANES, (base + k) // SUBLANES * SUBLANES + c0
            src = ext_cf[lo:lo + CONV_ROWS, :] if r == 0 else shifted[r, lo:lo + CONV_ROWS, :]
            acc = acc + src * wdw_ref[k:k + 1, :]
        conv_out[c0:c0 + CONV_ROWS, :] = acc
    y = conv_out[...] + bdw_ref[...]
    mu = jnp.mean(y, axis=-1, keepdims=True)
    yc = y - mu
    var = jnp.mean(yc * yc, axis=-1, keepdims=True)
    z = (yc * lax.rsqrt(var + EPS)) * lng_ref[...] + lnb_ref[...]
    z = z * jax.nn.sigmoid(z)
    y_cf = _dot(z.astype(jnp.bfloat16), wpw_ref[...])

    o_ref[0, :, 0:GROUP_W] = y_sc.astype(jnp.bfloat16)
    o_ref[0, :, GROUP_W:2 * GROUP_W] = y_cf.astype(jnp.bfloat16)


def _local(usc, ucf, wts, layer, t_len):
    b, l, _ = usc.shape
    nt = l // t_len
    hb = t_len // HALO
    nhb = l // HALO
    grid = (b, nt)
    main = lambda w: pl.BlockSpec((1, t_len, w), lambda i, j: (i, j, 0))
    prev = lambda w: pl.BlockSpec((1, HALO, w), lambda i, j: (i, jnp.maximum(j * hb - 1, 0), 0))
    nxt = lambda w: pl.BlockSpec((1, HALO, w), lambda i, j: (i, jnp.minimum((j + 1) * hb, nhb - 1), 0))
    names = ["w_sc", "cf_w_dw", "cf_b_dw", "cf_ln_g", "cf_ln_b", "cf_w_pw"]
    s_len = (HALO - CF_KERNEL // 2 + CF_KERNEL - 1) // SUBLANES * SUBLANES + t_len
    return pl.pallas_call(
        functools.partial(_local_kernel, t_len=t_len),
        grid=grid,
        in_specs=[main(SC_END), prev(SC_END), nxt(SC_END),
                  main(2 * GROUP_W), prev(2 * GROUP_W), nxt(2 * GROUP_W)]
                 + [_layer_spec(wts[n], layer) for n in names],
        out_specs=main(2 * GROUP_W),
        out_shape=jax.ShapeDtypeStruct((b, l, 2 * GROUP_W), jnp.bfloat16),
        scratch_shapes=[pltpu.VMEM((t_len + 2 * HALO, GROUP_W), jnp.float32),
                        pltpu.VMEM((t_len + 2 * HALO, GROUP_W), jnp.float32),
                        pltpu.VMEM((SUBLANES, s_len, GROUP_W), jnp.float32),
                        pltpu.VMEM((t_len, GROUP_W), jnp.float32)],
        compiler_params=_cparams(2), name="conv_mixers",
    )(usc, usc, usc, ucf, ucf, ucf, *[wts[n] for n in names])


def _outmlp_kernel(x_ref, yloc_ref, ymla_ref, yna_ref, mod_ref, gpm_ref, gpre_ref, gpost_ref,
                   wout_ref, w1_ref, w2_ref, o_ref):
    x = x_ref[0]
    ga1 = mod_ref[2:3, :]
    sh2 = mod_ref[3:4, :]
    sc2 = mod_ref[4:5, :]
    ga2 = mod_ref[5:6, :]
    ycat = jnp.concatenate([yloc_ref[0], ymla_ref[0], yna_ref[0]], axis=-1)
    y = _dot(ycat, wout_ref[...])
    x1 = x + ga1 * (_rms(y) * gpm_ref[...])
    hm = ((_rms(x1) * gpre_ref[...]) * (1.0 + sc2) + sh2).astype(jnp.bfloat16)
    acc = jnp.zeros(x.shape, jnp.float32)
    for c in range(MLP_HIDDEN // MLP_CHUNK):
        lo = c * MLP_CHUNK
        hj = jnp.maximum(_dot(hm, w1_ref[:, lo:lo + MLP_CHUNK]), 0.0)
        acc = acc + _dot((hj * hj).astype(jnp.bfloat16), w2_ref[lo:lo + MLP_CHUNK, :])
    o_ref[0] = x1 + ga2 * (_rms(acc) * gpost_ref[...])


def _outmlp(x, yloc, ymla, yna, mods, mod_row, wts, layer, tm):
    b, l, d = x.shape
    grid = (b, l // tm)
    tok = lambda w: pl.BlockSpec((1, tm, w), lambda i, j: (i, j, 0))
    gains = ["g_post_mix", "g_pre_mlp", "g_post_mlp"]
    mats = ["w_out", "w_mlp1", "w_mlp2"]
    return pl.pallas_call(
        _outmlp_kernel,
        grid=grid,
        in_specs=[tok(d), tok(2 * GROUP_W), tok(MLA_PAD), tok(GROUP_W), _mod_spec(mods, layer, mod_row)]
                 + [_layer_spec(wts[n], layer) for n in gains]
                 + [_layer_spec(wts[n], layer, single_buffer=True) for n in mats],
        out_specs=tok(d),
        out_shape=jax.ShapeDtypeStruct((b, l, d), jnp.float32),
        compiler_params=_cparams(2), name="outproj_mlp",
    )(x, yloc, ymla, yna, mods, *[wts[n] for n in gains], *[wts[n] for n in mats])


def _swap_halves_signed(w):
    q = MLA_ROPE // 4
    x1, x2, x3, x4 = (w[..., i * q:(i + 1) * q] for i in range(4))
    return jnp.concatenate([-x2, x1, -x4, x3], axis=-1)


def _prep_weights(p):
    bf16 = jnp.bfloat16
    w_in = p["w_in"]
    depth, d, _ = w_in.shape
    k_rope = w_in[..., MLA_Q_END + MLA_KV_RANK:MLA_KV_END]
    z32 = jnp.zeros((depth, d, MLA_ROPE), w_in.dtype)
    w_in_p = jnp.concatenate([
        w_in[..., :MLA_Q_END + MLA_KV_RANK],
        k_rope, z32, _swap_halves_signed(k_rope), z32,
        w_in[..., MLA_KV_END:]], axis=-1).astype(bf16)

    wq = p["w_uq"].reshape(depth, MLA_Q_RANK, N_HEADS, MLA_NOPE + MLA_ROPE)
    zq = jnp.zeros((depth, MLA_Q_RANK, N_HEADS, HEAD_PAD - MLA_NOPE - MLA_ROPE), wq.dtype)
    zn = jnp.zeros((depth, MLA_Q_RANK, N_HEADS, MLA_NOPE), wq.dtype)
    q_main = jnp.concatenate([wq, zq], axis=-1).reshape(depth, MLA_Q_RANK, MLA_PAD)
    q_swap = jnp.concatenate([zn, _swap_halves_signed(wq[..., MLA_NOPE:]), zq], axis=-1)
    w_uq_p = jnp.concatenate([q_main, q_swap.reshape(depth, MLA_Q_RANK, MLA_PAD)], axis=-1).astype(bf16)

    wkv = p["w_ukv"].reshape(depth, MLA_KV_RANK, N_HEADS, MLA_NOPE + MLA_V)
    zk = jnp.zeros((depth, MLA_KV_RANK, N_HEADS, HEAD_PAD - MLA_NOPE), wkv.dtype)
    k_main = jnp.concatenate([wkv[..., :MLA_NOPE], zk], axis=-1).reshape(depth, MLA_KV_RANK, MLA_PAD)
    v_main = jnp.concatenate([wkv[..., MLA_NOPE:], zk], axis=-1).reshape(depth, MLA_KV_RANK, MLA_PAD)
    w_ukv_p = jnp.concatenate([k_main, v_main], axis=-1).astype(bf16)

    wo = p["w_out"]
    wo_mla = wo[:, GROUP_W:2 * GROUP_W].reshape(depth, N_HEADS, MLA_V, d)
    wo_mla = jnp.concatenate([wo_mla, jnp.zeros((depth, N_HEADS, HEAD_PAD - MLA_V, d), wo.dtype)], axis=2)
    w_out_p = jnp.concatenate([wo[:, :GROUP_W], wo[:, 2 * GROUP_W:3 * GROUP_W],
                               wo_mla.reshape(depth, MLA_PAD, d), wo[:, 3 * GROUP_W:]], axis=1).astype(bf16)

    row = lambda a: a.reshape(depth, 1, -1)
    return {
        "g_pre_mix": row(p["g_pre_mix"]), "w_in": w_in_p, "g_q": row(p["g_q"]), "w_uq": w_uq_p,
        "g_kv": row(p["g_kv"]), "w_ukv": w_ukv_p, "w_sc": p["w_sc"], "cf_w_dw": p["cf_w_dw"],
        "cf_b_dw": row(p["cf_b_dw"]), "cf_ln_g": row(p["cf_ln_g"]), "cf_ln_b": row(p["cf_ln_b"]),
        "cf_w_pw": p["cf_w_pw"].astype(bf16), "w_out": w_out_p, "g_post_mix": row(p["g_post_mix"]),
        "g_pre_mlp": row(p["g_pre_mlp"]), "w_mlp1": p["w_mlp1"].astype(bf16),
        "w_mlp2": p["w_mlp2"].astype(bf16), "g_post_mlp": row(p["g_post_mlp"]),
        "na_bias": _na_bias(p["na_rpb"]),
    }


def _rope_tables(l, rotate):
    f32 = jnp.float32
    ones = jnp.ones((l, MLA_NOPE), f32)
    zpad = jnp.zeros((l, HEAD_PAD - MLA_NOPE - MLA_ROPE), f32)
    z32 = jnp.zeros((l, MLA_ROPE), f32)
    if rotate:
        rows = l // GRID_W
        n_freq = MLA_ROPE // 4
        inv_freq = ROPE_BASE ** (-jnp.arange(n_freq, dtype=f32) / n_freq)
        ang_r = jnp.arange(rows, dtype=f32)[:, None] * inv_freq
        ang_c = jnp.arange(GRID_W, dtype=f32)[:, None] * inv_freq
        by_row = lambda t: jnp.repeat(t, GRID_W, axis=0)
        by_col = lambda t: jnp.tile(t, (rows, 1))
        cos = jnp.concatenate([by_row(jnp.cos(ang_r))] * 2 + [by_col(jnp.cos(ang_c))] * 2, axis=-1)
        sin = jnp.concatenate([by_row(jnp.sin(ang_r))] * 2 + [by_col(jnp.sin(ang_c))] * 2, axis=-1)
    else:
        cos = jnp.ones((l, MLA_ROPE), f32)
        sin = z32
    return {"cq": jnp.concatenate([ones, cos, zpad], axis=-1),
            "sq": jnp.concatenate([jnp.zeros_like(ones), sin, zpad], axis=-1),
            "tk": jnp.concatenate([cos, z32, sin, z32], axis=-1)}


def kernel(x, c, ctx, c_ctx, w_mod, b_mod, g_pre_mix, w_in, w_sc, g_q, w_uq, g_kv, w_ukv, cf_w_dw,
           cf_b_dw, cf_ln_g, cf_ln_b, cf_w_pw, na_rpb, w_out, g_post_mix, g_pre_mlp, w_mlp1, w_mlp2,
           g_post_mlp):
    b, l, d = x.shape
    lc = ctx.shape[1]
    depth = w_mod.shape[0]
    wts = _prep_weights({
        "g_pre_mix": g_pre_mix, "w_in": w_in, "w_sc": w_sc, "g_q": g_q, "w_uq": w_uq, "g_kv": g_kv,
        "w_ukv": w_ukv, "cf_w_dw": cf_w_dw, "cf_b_dw": cf_b_dw, "cf_ln_g": cf_ln_g, "cf_ln_b": cf_ln_b,
        "cf_w_pw": cf_w_pw, "na_rpb": na_rpb, "w_out": w_out, "g_post_mix": g_post_mix,
        "g_pre_mlp": g_pre_mlp, "w_mlp1": w_mlp1, "w_mlp2": w_mlp2, "g_post_mlp": g_post_mlp})

    n_rows = 8
    cs = jnp.concatenate([c, c_ctx[None, :], jnp.zeros((n_rows - b - 1, d), c.dtype)], axis=0)
    mods = _modulation(cs, w_mod, b_mod).reshape(depth, n_rows, N_MOD, d)
    ctx_row = b
    tabs_lat = _rope_tables(l, True)
    tabs_ctx = _rope_tables(lc, False)

    tm = 512
    tq = 1024
    xc = ctx
    for i in range(depth):
        last = i == depth - 1
        usc, q, k, v, ucf, nq, nk, nv = _inproj(x, mods, None, wts, i, tabs_lat, tm)
        usc_c, q_c, k_c, v_c, ucf_c, nq_c, nk_c, nv_c = _inproj(xc, mods, ctx_row, wts, i, tabs_ctx, lc)

        y_loc = _local(usc, ucf, wts, i, tm)
        y_mla = _mla(q, k, v, k_c, v_c, tq)
        y_na = _na(nq, nk, nv, nk_c, nv_c, wts["na_bias"], i)
        x = _outmlp(x, y_loc, y_mla, y_na, mods, None, wts, i, tm)

        if not last:
            yc_loc = _local(usc_c, ucf_c, wts, i, lc)
            yc_mla = _mla(q_c, None, None, k_c, v_c, lc)
            yc_na = _na(nq_c, None, None, nk_c, nv_c, None, i)
            xc = _outmlp(xc, yc_loc, yc_mla, yc_na, mods, ctx_row, wts, i, lc)
    return x
```

```python
import functools

import jax
import jax.numpy as jnp
import numpy as np
from jax import lax
from jax.experimental import pallas as pl
from jax.experimental.pallas import tpu as pltpu

D_MODEL = 1024
GRID_W = 64
GROUP_W = D_MODEL // 4
HEAD_DIM = 64
N_HEADS = GROUP_W // HEAD_DIM
SC_KERNEL = 3
MLA_Q_RANK = 256
MLA_KV_RANK = 128
MLA_NOPE = 64
MLA_ROPE = 32
MLA_V = 64
MLA_SCALE = (MLA_NOPE + MLA_ROPE) ** -0.5
CF_KERNEL = 31
NA_ROWS = 8
NA_COLS = 16
NA_SCALE = HEAD_DIM ** -0.5
ROPE_BASE = 10000.0
MLP_HIDDEN = 4 * D_MODEL
N_MOD = 6
EPS = 1e-6

SC_END = 3 * GROUP_W
MLA_Q_END = SC_END + MLA_Q_RANK
MLA_KV_END = MLA_Q_END + MLA_KV_RANK + MLA_ROPE
CF_END = MLA_KV_END + 2 * GROUP_W
NA_Q_END = CF_END + GROUP_W
P_IN = NA_Q_END + 2 * GROUP_W

LANES = 128
SUBLANES = 8
HEAD_PAD = LANES
MLA_PAD = N_HEADS * HEAD_PAD
PC_SC = 0
PC_CQ = PC_SC + SC_END
PC_CKV = PC_CQ + MLA_Q_RANK
PC_ROPE = PC_CKV + MLA_KV_RANK
PC_CF = PC_ROPE + LANES
PC_NA = PC_CF + 2 * GROUP_W
P_PAD = PC_NA + 3 * GROUP_W

NA_QROWS = 4
NA_WROWS = NA_QROWS + NA_ROWS
NA_TQ = NA_QROWS * GRID_W
NA_TK = NA_WROWS * GRID_W
HALO = 16
CONV_ROWS = 64
MASK_VALUE = -1e30
MLA_KCHUNK = 2048
LOG2E = 1.4426950408889634
MLP_CHUNK = 1024

VMEM_LIMIT = 56 * 1024 * 1024


def _cparams(n_axes):
    return pltpu.CompilerParams(dimension_semantics=("arbitrary",) * n_axes,
                                vmem_limit_bytes=VMEM_LIMIT)


def _layer_spec(arr, layer, single_buffer=False):
    nd = arr.ndim - 1
    kwargs = {"pipeline_mode": pl.Buffered(1)} if single_buffer else {}
    return pl.BlockSpec((None,) + arr.shape[1:], lambda *_: (layer,) + (0,) * nd, **kwargs)


def _mod_spec(mods, layer, row):
    if row is None:
        return pl.BlockSpec((None, None) + mods.shape[2:], lambda i, j: (layer, i, 0, 0))
    return pl.BlockSpec((None, None) + mods.shape[2:], lambda i, j: (layer, row, 0, 0))


def _rms(x):
    return x * lax.rsqrt(jnp.mean(x * x, axis=-1, keepdims=True) + EPS)


def _dot(a, b):
    return jnp.dot(a, b, preferred_element_type=jnp.float32)


def _dot_nt(a, b):
    return lax.dot_general(a, b, (((1,), (1,)), ((), ())), preferred_element_type=jnp.float32)


def _mod_kernel(cs_ref, w_ref, b_ref, o_ref):
    cs = cs_ref[...]
    a = (cs * jax.nn.sigmoid(cs)).astype(jnp.bfloat16)
    o_ref[0] = _dot(a, w_ref[0].astype(jnp.bfloat16)) + b_ref[0]


def _modulation(cs, w_mod, b_mod):
    depth, d, n = w_mod.shape
    bn = 1024
    return pl.pallas_call(
        _mod_kernel,
        grid=(depth, n // bn),
        in_specs=[pl.BlockSpec(cs.shape, lambda l, j: (0, 0)),
                  pl.BlockSpec((1, d, bn), lambda l, j: (l, 0, j)),
                  pl.BlockSpec((1, 1, bn), lambda l, j: (l, 0, j))],
        out_specs=pl.BlockSpec((1, cs.shape[0], bn), lambda l, j: (l, 0, j)),
        out_shape=jax.ShapeDtypeStruct((depth, cs.shape[0], n), jnp.float32),
        compiler_params=_cparams(2), name="modulation",
    )(cs, w_mod, b_mod.reshape(depth, 1, n))


def _inproj_kernel(x_ref, xp_ref, xn_ref, mod_ref, gpre_ref, win_ref, gq_ref, wuq_ref, gkv_ref, wukv_ref,
                   cq_ref, sq_ref, tk_ref, wsc_ref, wdw_ref, bdw_ref, lng_ref, lnb_ref, wpw_ref,
                   yloc_ref, q_ref, k_ref, v_ref, nq_ref, nk_ref, nv_ref,
                   h_ext, ext_sc, ext_cf, shifted, conv_out, *, tm):
    j = pl.program_id(1)
    sh1 = mod_ref[0:1, :]
    sc1 = mod_ref[1:2, :]

    def modulated(x):
        return ((_rms(x) * gpre_ref[...]) * (1.0 + sc1) + sh1).astype(jnp.bfloat16)

    h_ext[0:HALO, :] = modulated(xp_ref[0])
    h_ext[HALO:HALO + tm, :] = modulated(x_ref[0])
    h_ext[HALO + tm:2 * HALO + tm, :] = modulated(xn_ref[0])
    h_all = h_ext[...]
    h_main = h_ext[HALO:HALO + tm, :]
    u_cf = _dot(h_all, win_ref[:, PC_CF:PC_NA])
    u_sc = _dot(h_all, win_ref[:, PC_SC:PC_CQ])
    u_mla = _dot(h_main, win_ref[:, PC_CQ:PC_CF])
    u_na = _dot(h_main, win_ref[:, PC_NA:P_PAD])

    row = lax.broadcasted_iota(jnp.int32, (tm + 2 * HALO, 1), 0)
    lo_row = jnp.where(j > 0, 0, HALO)
    hi_row = jnp.where(j < pl.num_programs(1) - 1, tm + 2 * HALO, tm + HALO)
    in_seq = (row >= lo_row) & (row < hi_row)
    ext_sc[...] = jnp.where(in_seq, u_sc[:, 2 * GROUP_W:3 * GROUP_W] * u_sc[:, 0:GROUP_W], 0.0)
    ext_cf[...] = jnp.where(in_seq, u_cf[:, 0:GROUP_W] * jax.nn.sigmoid(u_cf[:, GROUP_W:2 * GROUP_W]), 0.0)

    acc = jnp.zeros((tm, GROUP_W), jnp.float32)
    for k in range(SC_KERNEL):
        off = HALO - SC_KERNEL // 2 + k
        acc = acc + ext_sc[off:off + tm, :] * wsc_ref[k:k + 1, :]
    y_sc = u_sc[HALO:HALO + tm, GROUP_W:2 * GROUP_W] * acc

    base = HALO - CF_KERNEL // 2
    s_len = shifted.shape[1]
    for r in range(1, SUBLANES):
        shifted[r] = ext_cf[r:r + s_len, :]
    for c0 in range(0, tm, CONV_ROWS):
        acc = jnp.zeros((CONV_ROWS, GROUP_W), jnp.float32)
        for k in range(CF_KERNEL):
            r, lo = (base + k) % SUBLANES, (base + k) // SUBLANES * SUBLANES + c0
            src = ext_cf[lo:lo + CONV_ROWS, :] if r == 0 else shifted[r, lo:lo + CONV_ROWS, :]
            acc = acc + src * wdw_ref[k:k + 1, :]
        conv_out[c0:c0 + CONV_ROWS, :] = acc
    y = conv_out[...] + bdw_ref[...]
    mu = jnp.mean(y, axis=-1, keepdims=True)
    yc = y - mu
    var = jnp.mean(yc * yc, axis=-1, keepdims=True)
    z = (yc * lax.rsqrt(var + EPS)) * lng_ref[...] + lnb_ref[...]
    z = z * jax.nn.sigmoid(z)
    y_cf = _dot(z.astype(jnp.bfloat16), wpw_ref[...])
    yloc_ref[0, :, 0:GROUP_W] = y_sc.astype(jnp.bfloat16)
    yloc_ref[0, :, GROUP_W:2 * GROUP_W] = y_cf.astype(jnp.bfloat16)

    nq_ref[0] = u_na[:, 0:GROUP_W].astype(jnp.bfloat16)
    nk_ref[0] = u_na[:, GROUP_W:2 * GROUP_W].astype(jnp.bfloat16)
    nv_ref[0] = u_na[:, 2 * GROUP_W:3 * GROUP_W].astype(jnp.bfloat16)

    cqn = (_rms(u_mla[:, 0:MLA_Q_RANK]) * gq_ref[...]).astype(jnp.bfloat16)
    qf = _dot(cqn, wuq_ref[...])
    cq_t = cq_ref[...]
    sq_t = sq_ref[...]
    for hd in range(N_HEADS):
        lo = hd * HEAD_PAD
        qh = qf[:, lo:lo + HEAD_PAD] * cq_t + qf[:, MLA_PAD + lo:MLA_PAD + lo + HEAD_PAD] * sq_t
        q_ref[0, :, lo:lo + HEAD_PAD] = qh.astype(jnp.bfloat16)

    ckv_lo = MLA_Q_RANK
    rope_lo = MLA_Q_RANK + MLA_KV_RANK
    ckvn = (_rms(u_mla[:, ckv_lo:rope_lo]) * gkv_ref[...]).astype(jnp.bfloat16)
    kvf = _dot(ckvn, wukv_ref[...])
    g = u_mla[:, rope_lo:rope_lo + LANES] * tk_ref[...]
    lane = lax.broadcasted_iota(jnp.int32, g.shape, 1)
    rot = pltpu.roll(g, 2 * MLA_ROPE, axis=1) + g
    kr = jnp.where((lane >= MLA_NOPE) & (lane < MLA_NOPE + MLA_ROPE), rot, 0.0)
    for hd in range(N_HEADS):
        lo = hd * HEAD_PAD
        k_ref[0, :, lo:lo + HEAD_PAD] = (kvf[:, lo:lo + HEAD_PAD] + kr).astype(jnp.bfloat16)
        vh = kvf[:, MLA_PAD + lo:MLA_PAD + lo + HEAD_PAD]
        v_ref[0, :, lo:lo + HEAD_PAD] = jnp.where(lane == MLA_V, 1.0, vh).astype(jnp.bfloat16)


def _inproj(x, mods, mod_row, wts, layer, tabs, tm):
    b, l, d = x.shape
    grid = (b, l // tm)
    hb = tm // HALO
    nhb = l // HALO
    tok = lambda w: pl.BlockSpec((1, tm, w), lambda i, j: (i, j, 0))
    prev = pl.BlockSpec((1, HALO, d), lambda i, j: (i, jnp.maximum(j * hb - 1, 0), 0))
    nxt = pl.BlockSpec((1, HALO, d), lambda i, j: (i, jnp.minimum((j + 1) * hb, nhb - 1), 0))
    tab = pl.BlockSpec((tm, LANES), lambda i, j: (j, 0))
    f32, bf16 = jnp.float32, jnp.bfloat16
    widths = [2 * GROUP_W, MLA_PAD, MLA_PAD, MLA_PAD, GROUP_W, GROUP_W, GROUP_W]
    names = ["g_pre_mix", "w_in", "g_q", "w_uq", "g_kv", "w_ukv"]
    conv_names = ["w_sc", "cf_w_dw", "cf_b_dw", "cf_ln_g", "cf_ln_b", "cf_w_pw"]
    s_len = (HALO - CF_KERNEL // 2 + CF_KERNEL - 1) // SUBLANES * SUBLANES + tm
    return pl.pallas_call(
        functools.partial(_inproj_kernel, tm=tm),
        grid=grid,
        in_specs=[tok(d), prev, nxt, _mod_spec(mods, layer, mod_row)]
                 + [_layer_spec(wts[n], layer) for n in names] + [tab, tab, tab]
                 + [_layer_spec(wts[n], layer) for n in conv_names],
        out_specs=[tok(w) for w in widths],
        out_shape=[jax.ShapeDtypeStruct((b, l, w), bf16) for w in widths],
        scratch_shapes=[pltpu.VMEM((tm + 2 * HALO, d), bf16),
                        pltpu.VMEM((tm + 2 * HALO, GROUP_W), f32),
                        pltpu.VMEM((tm + 2 * HALO, GROUP_W), f32),
                        pltpu.VMEM((SUBLANES, s_len, GROUP_W), f32),
                        pltpu.VMEM((tm, GROUP_W), f32)],
        compiler_params=_cparams(2), name="inproj",
    )(x, x, x, mods, *[wts[n] for n in names], tabs["cq"], tabs["sq"], tabs["tk"],
      *[wts[n] for n in conv_names])


def _mla_kernel(*refs, has_lat):
    if has_lat:
        q_ref, kl_ref, vl_ref, kc_ref, vc_ref, o_ref = refs
    else:
        q_ref, kc_ref, vc_ref, o_ref = refs
    q = q_ref[0]
    chunks = [(kc_ref, vc_ref, 0, kc_ref.shape[1])]
    if has_lat:
        n_lat = kl_ref.shape[1]
        chunks += [(kl_ref, vl_ref, lo, MLA_KCHUNK) for lo in range(0, n_lat, MLA_KCHUNK)]
    c2 = MLA_SCALE * LOG2E
    m = o = None
    for k_ref, v_ref, lo, n in chunks:
        s = _dot_nt(q, k_ref[0, lo:lo + n, :])
        m_chunk = jnp.max(s, axis=-1, keepdims=True)
        m_new = m_chunk if m is None else jnp.maximum(m, m_chunk)
        p = jnp.exp2((s - m_new) * c2).astype(jnp.bfloat16)
        pv = _dot(p, v_ref[0, lo:lo + n, :])
        o = pv if o is None else o * jnp.exp2((m - m_new) * c2) + pv
        m = m_new
    o_ref[0] = (o * (1.0 / o[:, MLA_V:MLA_V + 1])).astype(jnp.bfloat16)


def _mla(q, k_lat, v_lat, k_ctx, v_ctx, tq):
    b, lq, _ = q.shape
    lc = k_ctx.shape[1]
    has_lat = k_lat is not None
    grid = (b, N_HEADS, lq // tq)
    qspec = pl.BlockSpec((1, tq, HEAD_PAD), lambda i, h, j: (i, j, h))
    kvspec = lambda n: pl.BlockSpec((1, n, HEAD_PAD), lambda i, h, j: (i, 0, h))
    in_specs = [qspec]
    args = [q]
    if has_lat:
        in_specs += [kvspec(k_lat.shape[1])] * 2
        args += [k_lat, v_lat]
    in_specs += [kvspec(lc)] * 2
    args += [k_ctx, v_ctx]
    return pl.pallas_call(
        functools.partial(_mla_kernel, has_lat=has_lat),
        grid=grid,
        in_specs=in_specs,
        out_specs=qspec,
        out_shape=jax.ShapeDtypeStruct((b, lq, MLA_PAD), jnp.bfloat16),
        compiler_params=_cparams(3), name="mla_attn",
    )(*args)


def _na_kernel(*refs, has_local):
    if has_local:
        q_ref, k_ref, v_ref, kc_ref, vc_ref, planes_ref, o_ref, bias_ref = refs

        @pl.when((pl.program_id(0) == 0) & (pl.program_id(1) == 0))
        def _():
            left = lax.broadcasted_iota(jnp.int32, (GRID_W, 2 * GRID_W), 1) < GRID_W
            for c, qi, kp in np.ndindex(3, NA_QROWS, NA_WROWS // 2):
                pa, pb = (int(v) for v in _NA_PLANE[c, qi, 2 * kp:2 * kp + 2])
                for hd in range(N_HEADS):
                    blk2 = planes_ref[hd, pa] if pa == pb else jnp.where(left, planes_ref[hd, pa],
                                                                         planes_ref[hd, pb])
                    bias_ref[c, hd, qi * GRID_W:(qi + 1) * GRID_W,
                             2 * kp * GRID_W:2 * (kp + 1) * GRID_W] = blk2

        blk = pl.program_id(1)
        nblk = pl.num_programs(1)
        wrow = jnp.clip(blk * NA_QROWS - NA_ROWS // 2, 0, GRID_W - NA_WROWS)
        start = pl.multiple_of(wrow * GRID_W, GRID_W)
        case = jnp.where(blk == 0, 0, jnp.where(blk == nblk - 1, 2, 1))
        kwin = k_ref[0, pl.ds(start, NA_TK), :]
        vwin = v_ref[0, pl.ds(start, NA_TK), :]
    else:
        q_ref, kc_ref, vc_ref, o_ref = refs
    q = q_ref[0]
    kc = kc_ref[0]
    vc = vc_ref[0]
    lane = lax.broadcasted_iota(jnp.int32, q.shape, 1)
    out = jnp.zeros(q.shape, jnp.float32)
    c2 = NA_SCALE * LOG2E
    for hd in range(N_HEADS):
        in_head = (lane >= hd * HEAD_DIM) & (lane < (hd + 1) * HEAD_DIM)
        qm = jnp.where(in_head, q, jnp.zeros_like(q))
        s_c = _dot_nt(qm, kc) * c2
        m = jnp.max(s_c, axis=-1, keepdims=True)
        if has_local:
            s_l = _dot_nt(qm, kwin) * c2 + bias_ref[case, hd]
            m = jnp.maximum(m, jnp.max(s_l, axis=-1, keepdims=True))
        p_c = jnp.exp2(s_c - m)
        den = jnp.sum(p_c, axis=-1, keepdims=True)
        o = _dot(p_c.astype(jnp.bfloat16), vc)
        if has_local:
            p_l = jnp.exp2(s_l - m)
            den = den + jnp.sum(p_l, axis=-1, keepdims=True)
            o = o + _dot(p_l.astype(jnp.bfloat16), vwin)
        out = jnp.where(in_head, o * (1.0 / den), out)
    o_ref[0] = out.astype(jnp.bfloat16)


def _na(q, k, v, k_ctx, v_ctx, planes, layer):
    b, l, w = q.shape
    lc = k_ctx.shape[1]
    has_local = k is not None
    tq = NA_TQ
    grid = (b, l // tq)
    qspec = pl.BlockSpec((1, tq, w), lambda i, j: (i, j, 0))
    full = lambda n: pl.BlockSpec((1, n, w), lambda i, j: (i, 0, 0))
    in_specs = [qspec]
    args = [q]
    if has_local:
        in_specs += [full(l), full(l)]
        args += [k, v]
    in_specs += [full(lc), full(lc)]
    args += [k_ctx, v_ctx]
    scratch = []
    if has_local:
        in_specs += [_layer_spec(planes, layer)]
        args += [planes]
        scratch = [pltpu.VMEM((3, N_HEADS, NA_TQ, NA_TK), jnp.float32)]
    return pl.pallas_call(
        functools.partial(_na_kernel, has_local=has_local),
        grid=grid,
        in_specs=in_specs,
        out_specs=qspec,
        out_shape=jax.ShapeDtypeStruct((b, l, w), jnp.bfloat16),
        scratch_shapes=scratch,
        compiler_params=_cparams(2), name="na_attn",
    )(*args)


NA_MASKED_PLANE = 2 * NA_ROWS - 1


def _na_plane_index():
    rows = GRID_W
    idx = np.full((3, NA_QROWS, NA_WROWS), NA_MASKED_PLANE, np.int32)
    for c, r0 in enumerate((0, NA_QROWS, rows - NA_QROWS)):
        wrow = min(max(r0 - NA_ROWS // 2, 0), rows - NA_WROWS)
        for qi in range(NA_QROWS):
            r = r0 + qi
            band = min(max(r - NA_ROWS // 2, 0), rows - NA_ROWS)
            for kj in range(NA_WROWS):
                kr = wrow + kj
                if band <= kr < band + NA_ROWS:
                    idx[c, qi, kj] = kr - r + (NA_ROWS - 1)
    return idx


_NA_PLANE = _na_plane_index()


def _na_planes(rpb):
    depth = rpb.shape[0]
    w = GRID_W
    n_off_r = 2 * NA_ROWS - 1
    delta = np.arange(2 * w - 1) - (w - 1)
    by_delta = rpb[..., np.clip(delta + (NA_COLS - 1), 0, 2 * NA_COLS - 2)]
    by_delta = jnp.pad(by_delta, ((0, 0), (0, 0), (0, 0), (0, 1)))
    tiled = jnp.broadcast_to(by_delta[..., None, :], (depth, N_HEADS, n_off_r, w, 2 * w))
    tc = tiled.reshape(depth, N_HEADS, n_off_r, 2 * w * w)[..., :w * (2 * w - 1)]
    tc = tc.reshape(depth, N_HEADS, n_off_r, w, 2 * w - 1)[..., w - 1:]
    col = np.arange(w)
    cstart = np.clip(col - NA_COLS // 2, 0, w - NA_COLS)
    col_ok = (col[None, :] >= cstart[:, None]) & (col[None, :] < cstart[:, None] + NA_COLS)
    tc = jnp.where(col_ok, tc * LOG2E, MASK_VALUE)
    masked = jnp.full((depth, N_HEADS, 1, w, w), MASK_VALUE, tc.dtype)
    planes = jnp.concatenate([tc, masked], axis=2).astype(jnp.float32)
    return jnp.concatenate([planes, planes], axis=-1)


def _outmlp_kernel(x_ref, yloc_ref, ymla_ref, yna_ref, mod_ref, gpm_ref, gpre_ref, gpost_ref,
                   wout_ref, w1_ref, w2_ref, o_ref):
    x = x_ref[0]
    ga1 = mod_ref[2:3, :]
    sh2 = mod_ref[3:4, :]
    sc2 = mod_ref[4:5, :]
    ga2 = mod_ref[5:6, :]
    ycat = jnp.concatenate([yloc_ref[0], ymla_ref[0], yna_ref[0]], axis=-1)
    y = _dot(ycat, wout_ref[...])
    x1 = x + ga1 * (_rms(y) * gpm_ref[...])
    hm = ((_rms(x1) * gpre_ref[...]) * (1.0 + sc2) + sh2).astype(jnp.bfloat16)
    acc = jnp.zeros(x.shape, jnp.float32)
    for c in range(MLP_HIDDEN // MLP_CHUNK):
        lo = c * MLP_CHUNK
        hj = jnp.maximum(_dot(hm, w1_ref[:, lo:lo + MLP_CHUNK]), 0.0)
        acc = acc + _dot((hj * hj).astype(jnp.bfloat16), w2_ref[lo:lo + MLP_CHUNK, :])
    o_ref[0] = x1 + ga2 * (_rms(acc) * gpost_ref[...])


def _outmlp(x, yloc, ymla, yna, mods, mod_row, wts, layer, tm):
    b, l, d = x.shape
    grid = (b, l // tm)
    tok = lambda w: pl.BlockSpec((1, tm, w), lambda i, j: (i, j, 0))
    gains = ["g_post_mix", "g_pre_mlp", "g_post_mlp"]
    mats = ["w_out", "w_mlp1", "w_mlp2"]
    return pl.pallas_call(
        _outmlp_kernel,
        grid=grid,
        in_specs=[tok(d), tok(2 * GROUP_W), tok(MLA_PAD), tok(GROUP_W), _mod_spec(mods, layer, mod_row)]
                 + [_layer_spec(wts[n], layer) for n in gains]
                 + [_layer_spec(wts[n], layer, single_buffer=True) for n in mats],
        out_specs=tok(d),
        out_shape=jax.ShapeDtypeStruct((b, l, d), jnp.float32),
        compiler_params=_cparams(2), name="outproj_mlp",
    )(x, yloc, ymla, yna, mods, *[wts[n] for n in gains], *[wts[n] for n in mats])


def _swap_halves_signed(w):
    q = MLA_ROPE // 4
    x1, x2, x3, x4 = (w[..., i * q:(i + 1) * q] for i in range(4))
    return jnp.concatenate([-x2, x1, -x4, x3], axis=-1)


def _prep_weights(p):
    bf16 = jnp.bfloat16
    w_in = p["w_in"]
    depth, d, _ = w_in.shape
    k_rope = w_in[..., MLA_Q_END + MLA_KV_RANK:MLA_KV_END]
    z32 = jnp.zeros((depth, d, MLA_ROPE), w_in.dtype)
    w_in_p = jnp.concatenate([
        w_in[..., :MLA_Q_END + MLA_KV_RANK],
        k_rope, z32, _swap_halves_signed(k_rope), z32,
        w_in[..., MLA_KV_END:]], axis=-1).astype(bf16)

    wq = p["w_uq"].reshape(depth, MLA_Q_RANK, N_HEADS, MLA_NOPE + MLA_ROPE)
    zq = jnp.zeros((depth, MLA_Q_RANK, N_HEADS, HEAD_PAD - MLA_NOPE - MLA_ROPE), wq.dtype)
    zn = jnp.zeros((depth, MLA_Q_RANK, N_HEADS, MLA_NOPE), wq.dtype)
    q_main = jnp.concatenate([wq, zq], axis=-1).reshape(depth, MLA_Q_RANK, MLA_PAD)
    q_swap = jnp.concatenate([zn, _swap_halves_signed(wq[..., MLA_NOPE:]), zq], axis=-1)
    w_uq_p = jnp.concatenate([q_main, q_swap.reshape(depth, MLA_Q_RANK, MLA_PAD)], axis=-1).astype(bf16)

    wkv = p["w_ukv"].reshape(depth, MLA_KV_RANK, N_HEADS, MLA_NOPE + MLA_V)
    zk = jnp.zeros((depth, MLA_KV_RANK, N_HEADS, HEAD_PAD - MLA_NOPE), wkv.dtype)
    k_main = jnp.concatenate([wkv[..., :MLA_NOPE], zk], axis=-1).reshape(depth, MLA_KV_RANK, MLA_PAD)
    v_main = jnp.concatenate([wkv[..., MLA_NOPE:], zk], axis=-1).reshape(depth, MLA_KV_RANK, MLA_PAD)
    w_ukv_p = jnp.concatenate([k_main, v_main], axis=-1).astype(bf16)

    wo = p["w_out"]
    wo_mla = wo[:, GROUP_W:2 * GROUP_W].reshape(depth, N_HEADS, MLA_V, d)
    wo_mla = jnp.concatenate([wo_mla, jnp.zeros((depth, N_HEADS, HEAD_PAD - MLA_V, d), wo.dtype)], axis=2)
    w_out_p = jnp.concatenate([wo[:, :GROUP_W], wo[:, 2 * GROUP_W:3 * GROUP_W],
                               wo_mla.reshape(depth, MLA_PAD, d), wo[:, 3 * GROUP_W:]], axis=1).astype(bf16)

    row = lambda a: a.reshape(depth, 1, -1)
    return {
        "g_pre_mix": row(p["g_pre_mix"]), "w_in": w_in_p, "g_q": row(p["g_q"]), "w_uq": w_uq_p,
        "g_kv": row(p["g_kv"]), "w_ukv": w_ukv_p, "w_sc": p["w_sc"], "cf_w_dw": p["cf_w_dw"],
        "cf_b_dw": row(p["cf_b_dw"]), "cf_ln_g": row(p["cf_ln_g"]), "cf_ln_b": row(p["cf_ln_b"]),
        "cf_w_pw": p["cf_w_pw"].astype(bf16), "w_out": w_out_p, "g_post_mix": row(p["g_post_mix"]),
        "g_pre_mlp": row(p["g_pre_mlp"]), "w_mlp1": p["w_mlp1"].astype(bf16),
        "w_mlp2": p["w_mlp2"].astype(bf16), "g_post_mlp": row(p["g_post_mlp"]),
        "na_planes": _na_planes(p["na_rpb"]),
    }


def _rope_tables(l, rotate):
    f32 = jnp.float32
    ones = jnp.ones((l, MLA_NOPE), f32)
    zpad = jnp.zeros((l, HEAD_PAD - MLA_NOPE - MLA_ROPE), f32)
    z32 = jnp.zeros((l, MLA_ROPE), f32)
    if rotate:
        rows = l // GRID_W
        n_freq = MLA_ROPE // 4
        inv_freq = ROPE_BASE ** (-jnp.arange(n_freq, dtype=f32) / n_freq)
        ang_r = jnp.arange(rows, dtype=f32)[:, None] * inv_freq
        ang_c = jnp.arange(GRID_W, dtype=f32)[:, None] * inv_freq
        by_row = lambda t: jnp.repeat(t, GRID_W, axis=0)
        by_col = lambda t: jnp.tile(t, (rows, 1))
        cos = jnp.concatenate([by_row(jnp.cos(ang_r))] * 2 + [by_col(jnp.cos(ang_c))] * 2, axis=-1)
        sin = jnp.concatenate([by_row(jnp.sin(ang_r))] * 2 + [by_col(jnp.sin(ang_c))] * 2, axis=-1)
    else:
        cos = jnp.ones((l, MLA_ROPE), f32)
        sin = z32
    return {"cq": jnp.concatenate([ones, cos, zpad], axis=-1),
            "sq": jnp.concatenate([jnp.zeros_like(ones), sin, zpad], axis=-1),
            "tk": jnp.concatenate([cos, z32, sin, z32], axis=-1)}


def kernel(x, c, ctx, c_ctx, w_mod, b_mod, g_pre_mix, w_in, w_sc, g_q, w_uq, g_kv, w_ukv, cf_w_dw,
           cf_b_dw, cf_ln_g, cf_ln_b, cf_w_pw, na_rpb, w_out, g_post_mix, g_pre_mlp, w_mlp1, w_mlp2,
           g_post_mlp):
    b, l, d = x.shape
    lc = ctx.shape[1]
    depth = w_mod.shape[0]
    wts = _prep_weights({
        "g_pre_mix": g_pre_mix, "w_in": w_in, "w_sc": w_sc, "g_q": g_q, "w_uq": w_uq, "g_kv": g_kv,
        "w_ukv": w_ukv, "cf_w_dw": cf_w_dw, "cf_b_dw": cf_b_dw, "cf_ln_g": cf_ln_g, "cf_ln_b": cf_ln_b,
        "cf_w_pw": cf_w_pw, "na_rpb": na_rpb, "w_out": w_out, "g_post_mix": g_post_mix,
        "g_pre_mlp": g_pre_mlp, "w_mlp1": w_mlp1, "w_mlp2": w_mlp2, "g_post_mlp": g_post_mlp})

    n_rows = 8
    cs = jnp.concatenate([c, c_ctx[None, :], jnp.zeros((n_rows - b - 1, d), c.dtype)], axis=0)
    mods = _modulation(cs, w_mod, b_mod).reshape(depth, n_rows, N_MOD, d)
    ctx_row = b
    tabs_lat = _rope_tables(l, True)
    tabs_ctx = _rope_tables(lc, False)

    tm = 512
    tq = 1024
    xc = ctx
    for i in range(depth):
        last = i == depth - 1
        y_loc, q, k, v, nq, nk, nv = _inproj(x, mods, None, wts, i, tabs_lat, tm)
        yc_loc, q_c, k_c, v_c, nq_c, nk_c, nv_c = _inproj(xc, mods, ctx_row, wts, i, tabs_ctx, lc)

        y_mla = _mla(q, k, v, k_c, v_c, tq)
        y_na = _na(nq, nk, nv, nk_c, nv_c, wts["na_planes"], i)
        x = _outmlp(x, y_loc, y_mla, y_na, mods, None, wts, i, tm)

        if not last:
            yc_mla = _mla(q_c, None, None, k_c, v_c, lc)
            yc_na = _na(nq_c, None, None, nk_c, nv_c, None, i)
            xc = _outmlp(xc, yc_loc, yc_mla, yc_na, mods, ctx_row, wts, i, lc)
    return x
```

```python
import functools

import jax
import jax.numpy as jnp
import numpy as np
from jax import lax
from jax.experimental import pallas as pl
from jax.experimental.pallas import tpu as pltpu

D_MODEL = 1024
GRID_W = 64
GROUP_W = D_MODEL // 4
HEAD_DIM = 64
N_HEADS = GROUP_W // HEAD_DIM
SC_KERNEL = 3
MLA_Q_RANK = 256
MLA_KV_RANK = 128
MLA_NOPE = 64
MLA_ROPE = 32
MLA_V = 64
MLA_SCALE = (MLA_NOPE + MLA_ROPE) ** -0.5
CF_KERNEL = 31
NA_ROWS = 8
NA_COLS = 16
NA_SCALE = HEAD_DIM ** -0.5
ROPE_BASE = 10000.0
MLP_HIDDEN = 4 * D_MODEL
N_MOD = 6
EPS = 1e-6

SC_END = 3 * GROUP_W
MLA_Q_END = SC_END + MLA_Q_RANK
MLA_KV_END = MLA_Q_END + MLA_KV_RANK + MLA_ROPE
CF_END = MLA_KV_END + 2 * GROUP_W
NA_Q_END = CF_END + GROUP_W
P_IN = NA_Q_END + 2 * GROUP_W

LANES = 128
SUBLANES = 8
HEAD_PAD = LANES
MLA_PAD = N_HEADS * HEAD_PAD
PC_SC = 0
PC_CQ = PC_SC + SC_END
PC_CKV = PC_CQ + MLA_Q_RANK
PC_ROPE = PC_CKV + MLA_KV_RANK
PC_CF = PC_ROPE + LANES
PC_NA = PC_CF + 2 * GROUP_W
P_PAD = PC_NA + 3 * GROUP_W

NA_QROWS = 4
NA_WROWS = NA_QROWS + NA_ROWS
NA_TQ = NA_QROWS * GRID_W
NA_TK = NA_WROWS * GRID_W
HALO = 16
CONV_ROWS = 64
MASK_VALUE = -1e30
MLA_KCHUNK = 2048
MLA_HEADS_PER_STEP = 2
LOG2E = 1.4426950408889634
MLP_CHUNK = 1024

VMEM_LIMIT = 56 * 1024 * 1024


def _cparams(n_axes, flags=None):
    return pltpu.CompilerParams(dimension_semantics=("arbitrary",) * n_axes,
                                vmem_limit_bytes=VMEM_LIMIT, flags=flags)


def _layer_spec(arr, layer, single_buffer=False):
    nd = arr.ndim - 1
    kwargs = {"pipeline_mode": pl.Buffered(1)} if single_buffer else {}
    return pl.BlockSpec((None,) + arr.shape[1:], lambda *_: (layer,) + (0,) * nd, **kwargs)


def _mod_spec(mods, layer, row):
    if row is None:
        return pl.BlockSpec((None, None) + mods.shape[2:], lambda i, j: (layer, i, 0, 0))
    return pl.BlockSpec((None, None) + mods.shape[2:], lambda i, j: (layer, row, 0, 0))


def _rms(x):
    return x * lax.rsqrt(jnp.mean(x * x, axis=-1, keepdims=True) + EPS)


def _dot(a, b):
    return jnp.dot(a, b, preferred_element_type=jnp.float32)


def _dot_nt(a, b):
    return lax.dot_general(a, b, (((1,), (1,)), ((), ())), preferred_element_type=jnp.float32)


def _mod_kernel(cs_ref, w_ref, b_ref, o_ref):
    cs = cs_ref[...]
    a = (cs * jax.nn.sigmoid(cs)).astype(jnp.bfloat16)
    o_ref[0] = _dot(a, w_ref[0].astype(jnp.bfloat16)) + b_ref[0]


def _modulation(cs, w_mod, b_mod):
    depth, d, n = w_mod.shape
    bn = 1024
    return pl.pallas_call(
        _mod_kernel,
        grid=(depth, n // bn),
        in_specs=[pl.BlockSpec(cs.shape, lambda l, j: (0, 0)),
                  pl.BlockSpec((1, d, bn), lambda l, j: (l, 0, j)),
                  pl.BlockSpec((1, 1, bn), lambda l, j: (l, 0, j))],
        out_specs=pl.BlockSpec((1, cs.shape[0], bn), lambda l, j: (l, 0, j)),
        out_shape=jax.ShapeDtypeStruct((depth, cs.shape[0], n), jnp.float32),
        compiler_params=_cparams(2), name="modulation",
    )(cs, w_mod, b_mod.reshape(depth, 1, n))


def _inproj_kernel(x_ref, xp_ref, xn_ref, mod_ref, gpre_ref, win_ref, gq_ref, wuq_ref, gkv_ref, wukv_ref,
                   cq_ref, sq_ref, tk_ref, wsc_ref, wdw_ref, bdw_ref, lng_ref, lnb_ref, wpw_ref,
                   yloc_ref, q_ref, k_ref, v_ref, nq_ref, nk_ref, nv_ref,
                   h_ext, ext_sc, ext_cf, shifted, conv_out, *, tm):
    j = pl.program_id(1)
    sh1 = mod_ref[0:1, :]
    sc1 = mod_ref[1:2, :]

    def modulated(x):
        return ((_rms(x) * gpre_ref[...]) * (1.0 + sc1) + sh1).astype(jnp.bfloat16)

    h_ext[0:HALO, :] = modulated(xp_ref[0])
    h_ext[HALO:HALO + tm, :] = modulated(x_ref[0])
    h_ext[HALO + tm:2 * HALO + tm, :] = modulated(xn_ref[0])
    h_all = h_ext[...]
    h_main = h_ext[HALO:HALO + tm, :]
    u_cf = _dot(h_all, win_ref[:, PC_CF:PC_NA])
    u_sc = _dot(h_all, win_ref[:, PC_SC:PC_CQ])
    u_mla = _dot(h_main, win_ref[:, PC_CQ:PC_CF])
    u_na = _dot(h_main, win_ref[:, PC_NA:P_PAD])

    row = lax.broadcasted_iota(jnp.int32, (tm + 2 * HALO, 1), 0)
    lo_row = jnp.where(j > 0, 0, HALO)
    hi_row = jnp.where(j < pl.num_programs(1) - 1, tm + 2 * HALO, tm + HALO)
    in_seq = (row >= lo_row) & (row < hi_row)
    ext_sc[...] = jnp.where(in_seq, u_sc[:, 2 * GROUP_W:3 * GROUP_W] * u_sc[:, 0:GROUP_W], 0.0)
    ext_cf[...] = jnp.where(in_seq, u_cf[:, 0:GROUP_W] * jax.nn.sigmoid(u_cf[:, GROUP_W:2 * GROUP_W]), 0.0)

    acc = jnp.zeros((tm, GROUP_W), jnp.float32)
    for k in range(SC_KERNEL):
        off = HALO - SC_KERNEL // 2 + k
        acc = acc + ext_sc[off:off + tm, :] * wsc_ref[k:k + 1, :]
    y_sc = u_sc[HALO:HALO + tm, GROUP_W:2 * GROUP_W] * acc

    base = HALO - CF_KERNEL // 2
    s_len = shifted.shape[1]
    for r in range(1, SUBLANES):
        shifted[r] = ext_cf[r:r + s_len, :]
    for c0 in range(0, tm, CONV_ROWS):
        acc = jnp.zeros((CONV_ROWS, GROUP_W), jnp.float32)
        for k in range(CF_KERNEL):
            r, lo = (base + k) % SUBLANES, (base + k) // SUBLANES * SUBLANES + c0
            src = ext_cf[lo:lo + CONV_ROWS, :] if r == 0 else shifted[r, lo:lo + CONV_ROWS, :]
            acc = acc + src * wdw_ref[k:k + 1, :]
        conv_out[c0:c0 + CONV_ROWS, :] = acc
    y = conv_out[...] + bdw_ref[...]
    mu = jnp.mean(y, axis=-1, keepdims=True)
    yc = y - mu
    var = jnp.mean(yc * yc, axis=-1, keepdims=True)
    z = (yc * lax.rsqrt(var + EPS)) * lng_ref[...] + lnb_ref[...]
    z = z * jax.nn.sigmoid(z)
    y_cf = _dot(z.astype(jnp.bfloat16), wpw_ref[...])
    yloc_ref[0, :, 0:GROUP_W] = y_sc.astype(jnp.bfloat16)
    yloc_ref[0, :, GROUP_W:2 * GROUP_W] = y_cf.astype(jnp.bfloat16)

    nq_ref[0] = u_na[:, 0:GROUP_W].astype(jnp.bfloat16)
    nk_ref[0] = u_na[:, GROUP_W:2 * GROUP_W].astype(jnp.bfloat16)
    nv_ref[0] = u_na[:, 2 * GROUP_W:3 * GROUP_W].astype(jnp.bfloat16)

    cqn = (_rms(u_mla[:, 0:MLA_Q_RANK]) * gq_ref[...]).astype(jnp.bfloat16)
    qf = _dot(cqn, wuq_ref[...])
    cq_t = cq_ref[...]
    sq_t = sq_ref[...]
    for hd in range(N_HEADS):
        lo = hd * HEAD_PAD
        qh = qf[:, lo:lo + HEAD_PAD] * cq_t + qf[:, MLA_PAD + lo:MLA_PAD + lo + HEAD_PAD] * sq_t
        q_ref[0, :, lo:lo + HEAD_PAD] = qh.astype(jnp.bfloat16)

    ckv_lo = MLA_Q_RANK
    rope_lo = MLA_Q_RANK + MLA_KV_RANK
    ckvn = (_rms(u_mla[:, ckv_lo:rope_lo]) * gkv_ref[...]).astype(jnp.bfloat16)
    kvf = _dot(ckvn, wukv_ref[...])
    g = u_mla[:, rope_lo:rope_lo + LANES] * tk_ref[...]
    lane = lax.broadcasted_iota(jnp.int32, g.shape, 1)
    rot = pltpu.roll(g, 2 * MLA_ROPE, axis=1) + g
    kr = jnp.where((lane >= MLA_NOPE) & (lane < MLA_NOPE + MLA_ROPE), rot, 0.0)
    for hd in range(N_HEADS):
        lo = hd * HEAD_PAD
        k_ref[0, :, lo:lo + HEAD_PAD] = (kvf[:, lo:lo + HEAD_PAD] + kr).astype(jnp.bfloat16)
        vh = kvf[:, MLA_PAD + lo:MLA_PAD + lo + HEAD_PAD]
        v_ref[0, :, lo:lo + HEAD_PAD] = jnp.where(lane == MLA_V, 1.0, vh).astype(jnp.bfloat16)


def _inproj(x, mods, mod_row, wts, layer, tabs, tm):
    b, l, d = x.shape
    grid = (b, l // tm)
    hb = tm // HALO
    nhb = l // HALO
    tok = lambda w: pl.BlockSpec((1, tm, w), lambda i, j: (i, j, 0))
    prev = pl.BlockSpec((1, HALO, d), lambda i, j: (i, jnp.maximum(j * hb - 1, 0), 0))
    nxt = pl.BlockSpec((1, HALO, d), lambda i, j: (i, jnp.minimum((j + 1) * hb, nhb - 1), 0))
    tab = pl.BlockSpec((tm, LANES), lambda i, j: (j, 0))
    f32, bf16 = jnp.float32, jnp.bfloat16
    widths = [2 * GROUP_W, MLA_PAD, MLA_PAD, MLA_PAD, GROUP_W, GROUP_W, GROUP_W]
    names = ["g_pre_mix", "w_in", "g_q", "w_uq", "g_kv", "w_ukv"]
    conv_names = ["w_sc", "cf_w_dw", "cf_b_dw", "cf_ln_g", "cf_ln_b", "cf_w_pw"]
    s_len = (HALO - CF_KERNEL // 2 + CF_KERNEL - 1) // SUBLANES * SUBLANES + tm
    return pl.pallas_call(
        functools.partial(_inproj_kernel, tm=tm),
        grid=grid,
        in_specs=[tok(d), prev, nxt, _mod_spec(mods, layer, mod_row)]
                 + [_layer_spec(wts[n], layer) for n in names] + [tab, tab, tab]
                 + [_layer_spec(wts[n], layer) for n in conv_names],
        out_specs=[tok(w) for w in widths],
        out_shape=[jax.ShapeDtypeStruct((b, l, w), bf16) for w in widths],
        scratch_shapes=[pltpu.VMEM((tm + 2 * HALO, d), bf16),
                        pltpu.VMEM((tm + 2 * HALO, GROUP_W), f32),
                        pltpu.VMEM((tm + 2 * HALO, GROUP_W), f32),
                        pltpu.VMEM((SUBLANES, s_len, GROUP_W), f32),
                        pltpu.VMEM((tm, GROUP_W), f32)],
        compiler_params=_cparams(2), name="inproj",
    )(x, x, x, mods, *[wts[n] for n in names], tabs["cq"], tabs["sq"], tabs["tk"],
      *[wts[n] for n in conv_names])


def _mla_kernel(*refs, has_lat):
    if has_lat:
        q_ref, kl_ref, vl_ref, kc_ref, vc_ref, o_ref = refs
    else:
        q_ref, kc_ref, vc_ref, o_ref = refs
    chunks = [(kc_ref, vc_ref, 0, kc_ref.shape[1])]
    if has_lat:
        n_lat = kl_ref.shape[1]
        chunks += [(kl_ref, vl_ref, lo, MLA_KCHUNK) for lo in range(0, n_lat, MLA_KCHUNK)]
    c2 = MLA_SCALE * LOG2E
    outs = []
    for hd in range(MLA_HEADS_PER_STEP):
        lanes = slice(hd * HEAD_PAD, (hd + 1) * HEAD_PAD)
        q = q_ref[0, :, lanes]
        m = o = None
        for k_ref, v_ref, lo, n in chunks:
            s = _dot_nt(q, k_ref[0, lo:lo + n, lanes])
            m_chunk = jnp.max(s, axis=-1, keepdims=True)
            m_new = m_chunk if m is None else jnp.maximum(m, m_chunk)
            p = jnp.exp2((s - m_new) * c2).astype(jnp.bfloat16)
            pv = _dot(p, v_ref[0, lo:lo + n, lanes])
            o = pv if o is None else o * jnp.exp2((m - m_new) * c2) + pv
            m = m_new
        outs.append(o * (1.0 / o[:, MLA_V:MLA_V + 1]))
    lane = lax.broadcasted_iota(jnp.int32, outs[0].shape, 1)
    packed = jnp.where(lane < MLA_V, outs[0], pltpu.roll(outs[1], MLA_V, axis=1))
    o_ref[0] = packed.astype(jnp.bfloat16)


def _mla(q, k_lat, v_lat, k_ctx, v_ctx, tq):
    b, lq, _ = q.shape
    lc = k_ctx.shape[1]
    has_lat = k_lat is not None
    hw = MLA_HEADS_PER_STEP * HEAD_PAD
    grid = (b, N_HEADS // MLA_HEADS_PER_STEP, lq // tq)
    qspec = pl.BlockSpec((1, tq, hw), lambda i, h, j: (i, j, h))
    kvspec = lambda n: pl.BlockSpec((1, n, hw), lambda i, h, j: (i, 0, h))
    in_specs = [qspec]
    args = [q]
    if has_lat:
        in_specs += [kvspec(k_lat.shape[1])] * 2
        args += [k_lat, v_lat]
    in_specs += [kvspec(lc)] * 2
    args += [k_ctx, v_ctx]
    return pl.pallas_call(
        functools.partial(_mla_kernel, has_lat=has_lat),
        grid=grid,
        in_specs=in_specs,
        out_specs=pl.BlockSpec((1, tq, MLA_HEADS_PER_STEP * MLA_V), lambda i, h, j: (i, j, h)),
        out_shape=jax.ShapeDtypeStruct((b, lq, N_HEADS * MLA_V), jnp.bfloat16),
        compiler_params=_cparams(3), name="mla_attn",
    )(*args)


def _na_kernel(*refs, has_local):
    if has_local:
        q_ref, k_ref, v_ref, kc_ref, vc_ref, planes_ref, o_ref, bias_ref = refs

        @pl.when((pl.program_id(0) == 0) & (pl.program_id(1) == 0))
        def _():
            left = lax.broadcasted_iota(jnp.int32, (GRID_W, 2 * GRID_W), 1) < GRID_W
            for c, qi, kp in np.ndindex(3, NA_QROWS, NA_WROWS // 2):
                pa, pb = (int(v) for v in _NA_PLANE[c, qi, 2 * kp:2 * kp + 2])
                for hd in range(N_HEADS):
                    blk2 = planes_ref[hd, pa] if pa == pb else jnp.where(left, planes_ref[hd, pa],
                                                                         planes_ref[hd, pb])
                    bias_ref[c, hd, qi * GRID_W:(qi + 1) * GRID_W,
                             2 * kp * GRID_W:2 * (kp + 1) * GRID_W] = blk2

        blk = pl.program_id(1)
        nblk = pl.num_programs(1)
        wrow = jnp.clip(blk * NA_QROWS - NA_ROWS // 2, 0, GRID_W - NA_WROWS)
        start = pl.multiple_of(wrow * GRID_W, GRID_W)
        case = jnp.where(blk == 0, 0, jnp.where(blk == nblk - 1, 2, 1))
        kwin = k_ref[0, pl.ds(start, NA_TK), :]
        vwin = v_ref[0, pl.ds(start, NA_TK), :]
    else:
        q_ref, kc_ref, vc_ref, o_ref = refs
    q = q_ref[0]
    kc = kc_ref[0]
    vc = vc_ref[0]
    lane = lax.broadcasted_iota(jnp.int32, q.shape, 1)
    out = jnp.zeros(q.shape, jnp.float32)
    c2 = NA_SCALE * LOG2E
    for hd in range(N_HEADS):
        in_head = (lane >= hd * HEAD_DIM) & (lane < (hd + 1) * HEAD_DIM)
        qm = jnp.where(in_head, q, jnp.zeros_like(q))
        s_c = _dot_nt(qm, kc) * c2
        m = jnp.max(s_c, axis=-1, keepdims=True)
        if has_local:
            s_l = _dot_nt(qm, kwin) * c2 + bias_ref[case, hd]
            m = jnp.maximum(m, jnp.max(s_l, axis=-1, keepdims=True))
        p_c = jnp.exp2(s_c - m)
        den = jnp.sum(p_c, axis=-1, keepdims=True)
        o = _dot(p_c.astype(jnp.bfloat16), vc)
        if has_local:
            p_l = jnp.exp2(s_l - m)
            den = den + jnp.sum(p_l, axis=-1, keepdims=True)
            o = o + _dot(p_l.astype(jnp.bfloat16), vwin)
        out = jnp.where(in_head, o * (1.0 / den), out)
    o_ref[0] = out.astype(jnp.bfloat16)


def _na(q, k, v, k_ctx, v_ctx, planes, layer):
    b, l, w = q.shape
    lc = k_ctx.shape[1]
    has_local = k is not None
    tq = NA_TQ
    grid = (b, l // tq)
    qspec = pl.BlockSpec((1, tq, w), lambda i, j: (i, j, 0))
    full = lambda n: pl.BlockSpec((1, n, w), lambda i, j: (i, 0, 0))
    in_specs = [qspec]
    args = [q]
    if has_local:
        in_specs += [full(l), full(l)]
        args += [k, v]
    in_specs += [full(lc), full(lc)]
    args += [k_ctx, v_ctx]
    scratch = []
    if has_local:
        in_specs += [_layer_spec(planes, layer)]
        args += [planes]
        scratch = [pltpu.VMEM((3, N_HEADS, NA_TQ, NA_TK), jnp.float32)]
    return pl.pallas_call(
        functools.partial(_na_kernel, has_local=has_local),
        grid=grid,
        in_specs=in_specs,
        out_specs=qspec,
        out_shape=jax.ShapeDtypeStruct((b, l, w), jnp.bfloat16),
        scratch_shapes=scratch,
        compiler_params=_cparams(2), name="na_attn",
    )(*args)


NA_MASKED_PLANE = 2 * NA_ROWS - 1


def _na_plane_index():
    rows = GRID_W
    idx = np.full((3, NA_QROWS, NA_WROWS), NA_MASKED_PLANE, np.int32)
    for c, r0 in enumerate((0, NA_QROWS, rows - NA_QROWS)):
        wrow = min(max(r0 - NA_ROWS // 2, 0), rows - NA_WROWS)
        for qi in range(NA_QROWS):
            r = r0 + qi
            band = min(max(r - NA_ROWS // 2, 0), rows - NA_ROWS)
            for kj in range(NA_WROWS):
                kr = wrow + kj
                if band <= kr < band + NA_ROWS:
                    idx[c, qi, kj] = kr - r + (NA_ROWS - 1)
    return idx


_NA_PLANE = _na_plane_index()


def _na_planes(rpb):
    depth = rpb.shape[0]
    w = GRID_W
    n_off_r = 2 * NA_ROWS - 1
    delta = np.arange(2 * w - 1) - (w - 1)
    by_delta = rpb[..., np.clip(delta + (NA_COLS - 1), 0, 2 * NA_COLS - 2)]
    by_delta = jnp.pad(by_delta, ((0, 0), (0, 0), (0, 0), (0, 1)))
    tiled = jnp.broadcast_to(by_delta[..., None, :], (depth, N_HEADS, n_off_r, w, 2 * w))
    tc = tiled.reshape(depth, N_HEADS, n_off_r, 2 * w * w)[..., :w * (2 * w - 1)]
    tc = tc.reshape(depth, N_HEADS, n_off_r, w, 2 * w - 1)[..., w - 1:]
    col = np.arange(w)
    cstart = np.clip(col - NA_COLS // 2, 0, w - NA_COLS)
    col_ok = (col[None, :] >= cstart[:, None]) & (col[None, :] < cstart[:, None] + NA_COLS)
    tc = jnp.where(col_ok, tc * LOG2E, MASK_VALUE)
    masked = jnp.full((depth, N_HEADS, 1, w, w), MASK_VALUE, tc.dtype)
    planes = jnp.concatenate([tc, masked], axis=2).astype(jnp.float32)
    return jnp.concatenate([planes, planes], axis=-1)


def _outmlp_kernel(x_ref, yloc_ref, ymla_ref, yna_ref, mod_ref, gpm_ref, gpre_ref, gpost_ref,
                   wout_ref, w1_ref, w2_ref, o_ref):
    x = x_ref[0]
    ga1 = mod_ref[2:3, :]
    sh2 = mod_ref[3:4, :]
    sc2 = mod_ref[4:5, :]
    ga2 = mod_ref[5:6, :]
    ycat = jnp.concatenate([yloc_ref[0, :, 0:GROUP_W], ymla_ref[0], yloc_ref[0, :, GROUP_W:2 * GROUP_W],
                            yna_ref[0]], axis=-1)
    y = _dot(ycat, wout_ref[...])
    x1 = x + ga1 * (_rms(y) * gpm_ref[...])
    hm = ((_rms(x1) * gpre_ref[...]) * (1.0 + sc2) + sh2).astype(jnp.bfloat16)
    acc = jnp.zeros(x.shape, jnp.float32)
    for c in range(MLP_HIDDEN // MLP_CHUNK):
        lo = c * MLP_CHUNK
        hj = jnp.maximum(_dot(hm, w1_ref[:, lo:lo + MLP_CHUNK]), 0.0)
        acc = acc + _dot((hj * hj).astype(jnp.bfloat16), w2_ref[lo:lo + MLP_CHUNK, :])
    o_ref[0] = x1 + ga2 * (_rms(acc) * gpost_ref[...])


def _outmlp(x, yloc, ymla, yna, mods, mod_row, wts, layer, tm):
    b, l, d = x.shape
    grid = (b, l // tm)
    tok = lambda w: pl.BlockSpec((1, tm, w), lambda i, j: (i, j, 0))
    gains = ["g_post_mix", "g_pre_mlp", "g_post_mlp"]
    mats = ["w_out", "w_mlp1", "w_mlp2"]
    return pl.pallas_call(
        _outmlp_kernel,
        grid=grid,
        in_specs=[tok(d), tok(2 * GROUP_W), tok(GROUP_W), tok(GROUP_W), _mod_spec(mods, layer, mod_row)]
                 + [_layer_spec(wts[n], layer) for n in gains]
                 + [_layer_spec(wts[n], layer, single_buffer=True) for n in mats],
        out_specs=tok(d),
        out_shape=jax.ShapeDtypeStruct((b, l, d), jnp.float32),
        compiler_params=_cparams(2), name="outproj_mlp",
    )(x, yloc, ymla, yna, mods, *[wts[n] for n in gains], *[wts[n] for n in mats])


def _swap_halves_signed(w):
    q = MLA_ROPE // 4
    x1, x2, x3, x4 = (w[..., i * q:(i + 1) * q] for i in range(4))
    return jnp.concatenate([-x2, x1, -x4, x3], axis=-1)


def _prep_weights(p):
    bf16 = jnp.bfloat16
    w_in = p["w_in"]
    depth, d, _ = w_in.shape
    k_rope = w_in[..., MLA_Q_END + MLA_KV_RANK:MLA_KV_END]
    z32 = jnp.zeros((depth, d, MLA_ROPE), w_in.dtype)
    w_in_p = jnp.concatenate([
        w_in[..., :MLA_Q_END + MLA_KV_RANK],
        k_rope, z32, _swap_halves_signed(k_rope), z32,
        w_in[..., MLA_KV_END:]], axis=-1).astype(bf16)

    wq = p["w_uq"].reshape(depth, MLA_Q_RANK, N_HEADS, MLA_NOPE + MLA_ROPE)
    zq = jnp.zeros((depth, MLA_Q_RANK, N_HEADS, HEAD_PAD - MLA_NOPE - MLA_ROPE), wq.dtype)
    zn = jnp.zeros((depth, MLA_Q_RANK, N_HEADS, MLA_NOPE), wq.dtype)
    q_main = jnp.concatenate([wq, zq], axis=-1).reshape(depth, MLA_Q_RANK, MLA_PAD)
    q_swap = jnp.concatenate([zn, _swap_halves_signed(wq[..., MLA_NOPE:]), zq], axis=-1)
    w_uq_p = jnp.concatenate([q_main, q_swap.reshape(depth, MLA_Q_RANK, MLA_PAD)], axis=-1).astype(bf16)

    wkv = p["w_ukv"].reshape(depth, MLA_KV_RANK, N_HEADS, MLA_NOPE + MLA_V)
    zk = jnp.zeros((depth, MLA_KV_RANK, N_HEADS, HEAD_PAD - MLA_NOPE), wkv.dtype)
    k_main = jnp.concatenate([wkv[..., :MLA_NOPE], zk], axis=-1).reshape(depth, MLA_KV_RANK, MLA_PAD)
    v_main = jnp.concatenate([wkv[..., MLA_NOPE:], zk], axis=-1).reshape(depth, MLA_KV_RANK, MLA_PAD)
    w_ukv_p = jnp.concatenate([k_main, v_main], axis=-1).astype(bf16)

    row = lambda a: a.reshape(depth, 1, -1)
    return {
        "g_pre_mix": row(p["g_pre_mix"]), "w_in": w_in_p, "g_q": row(p["g_q"]), "w_uq": w_uq_p,
        "g_kv": row(p["g_kv"]), "w_ukv": w_ukv_p, "w_sc": p["w_sc"], "cf_w_dw": p["cf_w_dw"],
        "cf_b_dw": row(p["cf_b_dw"]), "cf_ln_g": row(p["cf_ln_g"]), "cf_ln_b": row(p["cf_ln_b"]),
        "cf_w_pw": p["cf_w_pw"].astype(bf16), "w_out": p["w_out"].astype(bf16), "g_post_mix": row(p["g_post_mix"]),
        "g_pre_mlp": row(p["g_pre_mlp"]), "w_mlp1": p["w_mlp1"].astype(bf16),
        "w_mlp2": p["w_mlp2"].astype(bf16), "g_post_mlp": row(p["g_post_mlp"]),
        "na_planes": _na_planes(p["na_rpb"]),
    }


def _rope_tables(l, rotate):
    f32 = np.float32
    ones = np.ones((l, MLA_NOPE), f32)
    zpad = np.zeros((l, HEAD_PAD - MLA_NOPE - MLA_ROPE), f32)
    z32 = np.zeros((l, MLA_ROPE), f32)
    if rotate:
        pos = np.arange(l)
        n_freq = MLA_ROPE // 4
        inv_freq = (f32(ROPE_BASE) ** (-np.arange(n_freq, dtype=f32) / f32(n_freq))).astype(f32)
        ang_row = (pos // GRID_W).astype(f32)[:, None] * inv_freq
        ang_col = (pos % GRID_W).astype(f32)[:, None] * inv_freq
        cos = np.concatenate([np.cos(ang_row)] * 2 + [np.cos(ang_col)] * 2, axis=-1).astype(f32)
        sin = np.concatenate([np.sin(ang_row)] * 2 + [np.sin(ang_col)] * 2, axis=-1).astype(f32)
    else:
        cos = np.ones((l, MLA_ROPE), f32)
        sin = z32
    return {"cq": jnp.asarray(np.concatenate([ones, cos, zpad], axis=-1)),
            "sq": jnp.asarray(np.concatenate([np.zeros_like(ones), sin, zpad], axis=-1)),
            "tk": jnp.asarray(np.concatenate([cos, z32, sin, z32], axis=-1))}


def kernel(x, c, ctx, c_ctx, w_mod, b_mod, g_pre_mix, w_in, w_sc, g_q, w_uq, g_kv, w_ukv, cf_w_dw,
           cf_b_dw, cf_ln_g, cf_ln_b, cf_w_pw, na_rpb, w_out, g_post_mix, g_pre_mlp, w_mlp1, w_mlp2,
           g_post_mlp):
    b, l, d = x.shape
    lc = ctx.shape[1]
    depth = w_mod.shape[0]
    wts = _prep_weights({
        "g_pre_mix": g_pre_mix, "w_in": w_in, "w_sc": w_sc, "g_q": g_q, "w_uq": w_uq, "g_kv": g_kv,
        "w_ukv": w_ukv, "cf_w_dw": cf_w_dw, "cf_b_dw": cf_b_dw, "cf_ln_g": cf_ln_g, "cf_ln_b": cf_ln_b,
        "cf_w_pw": cf_w_pw, "na_rpb": na_rpb, "w_out": w_out, "g_post_mix": g_post_mix,
        "g_pre_mlp": g_pre_mlp, "w_mlp1": w_mlp1, "w_mlp2": w_mlp2, "g_post_mlp": g_post_mlp})

    n_rows = 8
    cs = jnp.concatenate([c, c_ctx[None, :], jnp.zeros((n_rows - b - 1, d), c.dtype)], axis=0)
    mods = _modulation(cs, w_mod, b_mod).reshape(depth, n_rows, N_MOD, d)
    ctx_row = b
    tabs_lat = _rope_tables(l, True)
    tabs_ctx = _rope_tables(lc, False)

    tm = 512
    tq = 1024
    xc = ctx
    for i in range(depth):
        last = i == depth - 1
        y_loc, q, k, v, nq, nk, nv = _inproj(x, mods, None, wts, i, tabs_lat, tm)
        yc_loc, q_c, k_c, v_c, nq_c, nk_c, nv_c = _inproj(xc, mods, ctx_row, wts, i, tabs_ctx, lc)

        y_mla = _mla(q, k, v, k_c, v_c, tq)
        y_na = _na(nq, nk, nv, nk_c, nv_c, wts["na_planes"], i)
        x = _outmlp(x, y_loc, y_mla, y_na, mods, None, wts, i, tm)

        if not last:
            yc_mla = _mla(q_c, None, None, k_c, v_c, lc)
            yc_na = _na(nq_c, None, None, nk_c, nv_c, None, i)
            xc = _outmlp(xc, yc_loc, yc_mla, yc_na, mods, ctx_row, wts, i, lc)
    return x
```

```python
import functools

import jax
import jax.numpy as jnp
import numpy as np
from jax import lax
from jax.experimental import pallas as pl
from jax.experimental.pallas import tpu as pltpu

D_MODEL = 1024
GRID_W = 64
GROUP_W = D_MODEL // 4
HEAD_DIM = 64
N_HEADS = GROUP_W // HEAD_DIM
SC_KERNEL = 3
MLA_Q_RANK = 256
MLA_KV_RANK = 128
MLA_NOPE = 64
MLA_ROPE = 32
MLA_V = 64
MLA_SCALE = (MLA_NOPE + MLA_ROPE) ** -0.5
CF_KERNEL = 31
NA_ROWS = 8
NA_COLS = 16
NA_SCALE = HEAD_DIM ** -0.5
ROPE_BASE = 10000.0
MLP_HIDDEN = 4 * D_MODEL
N_MOD = 6
EPS = 1e-6

SC_END = 3 * GROUP_W
MLA_Q_END = SC_END + MLA_Q_RANK
MLA_KV_END = MLA_Q_END + MLA_KV_RANK + MLA_ROPE
CF_END = MLA_KV_END + 2 * GROUP_W
NA_Q_END = CF_END + GROUP_W
P_IN = NA_Q_END + 2 * GROUP_W

LANES = 128
SUBLANES = 8
HEAD_PAD = LANES
MLA_PAD = N_HEADS * HEAD_PAD
PC_SC = 0
PC_CQ = PC_SC + SC_END
PC_CKV = PC_CQ + MLA_Q_RANK
PC_ROPE = PC_CKV + MLA_KV_RANK
PC_CF = PC_ROPE + LANES
PC_NA = PC_CF + 2 * GROUP_W
P_PAD = PC_NA + 3 * GROUP_W

NA_QROWS = 4
NA_WROWS = NA_QROWS + NA_ROWS
NA_TQ = NA_QROWS * GRID_W
NA_TK = NA_WROWS * GRID_W
HALO = 16
CONV_ROWS = 64
MASK_VALUE = -1e30
MLA_KCHUNK = 2048
MLA_HEADS_PER_STEP = 2
LOG2E = 1.4426950408889634
MLP_CHUNK = 1024

VMEM_LIMIT = 56 * 1024 * 1024


def _cparams(n_axes):
    return pltpu.CompilerParams(dimension_semantics=("arbitrary",) * n_axes,
                                vmem_limit_bytes=VMEM_LIMIT)


def _layer_spec(arr, layer, single_buffer=False):
    nd = arr.ndim - 1
    kwargs = {"pipeline_mode": pl.Buffered(1)} if single_buffer else {}
    return pl.BlockSpec((None,) + arr.shape[1:], lambda *_: (layer,) + (0,) * nd, **kwargs)


def _mod_spec(mods, layer, row):
    if row is None:
        return pl.BlockSpec((None, None) + mods.shape[2:], lambda i, j: (layer, i, 0, 0))
    return pl.BlockSpec((None, None) + mods.shape[2:], lambda i, j: (layer, row, 0, 0))


def _rms(x):
    return x * lax.rsqrt(jnp.mean(x * x, axis=-1, keepdims=True) + EPS)


def _dot(a, b):
    return jnp.dot(a, b, preferred_element_type=jnp.float32)


def _dot_nt(a, b):
    return lax.dot_general(a, b, (((1,), (1,)), ((), ())), preferred_element_type=jnp.float32)


def _mod_kernel(cs_ref, w_ref, b_ref, o_ref):
    cs = cs_ref[...]
    a = (cs * jax.nn.sigmoid(cs)).astype(jnp.bfloat16)
    o_ref[0] = _dot(a, w_ref[0].astype(jnp.bfloat16)) + b_ref[0]


def _modulation(cs, w_mod, b_mod):
    depth, d, n = w_mod.shape
    bn = 1024
    return pl.pallas_call(
        _mod_kernel,
        grid=(depth, n // bn),
        in_specs=[pl.BlockSpec(cs.shape, lambda l, j: (0, 0)),
                  pl.BlockSpec((1, d, bn), lambda l, j: (l, 0, j)),
                  pl.BlockSpec((1, 1, bn), lambda l, j: (l, 0, j))],
        out_specs=pl.BlockSpec((1, cs.shape[0], bn), lambda l, j: (l, 0, j)),
        out_shape=jax.ShapeDtypeStruct((depth, cs.shape[0], n), jnp.float32),
        compiler_params=_cparams(2), name="modulation",
    )(cs, w_mod, b_mod.reshape(depth, 1, n))


def _inproj_kernel(x_ref, xp_ref, xn_ref, mod_ref, gpre_ref, win_ref, gq_ref, wuq_ref, gkv_ref, wukv_ref,
                   cq_ref, sq_ref, tk_ref, wsc_ref, wdw_ref, bdw_ref, lng_ref, lnb_ref, wpw_ref,
                   yloc_ref, q_ref, k_ref, v_ref, nq_ref, nk_ref, nv_ref,
                   h_ext, ext_sc, ext_cf, shifted, conv_out, *, tm):
    j = pl.program_id(1)
    sh1 = mod_ref[0:1, :]
    sc1 = mod_ref[1:2, :]

    gain = gpre_ref[...] * (1.0 + sc1)

    def modulated(x):
        return (_rms(x) * gain + sh1).astype(jnp.bfloat16)

    h_ext[0:HALO, :] = modulated(xp_ref[0])
    h_ext[HALO:HALO + tm, :] = modulated(x_ref[0])
    h_ext[HALO + tm:2 * HALO + tm, :] = modulated(xn_ref[0])
    h_all = h_ext[...]
    h_main = h_ext[HALO:HALO + tm, :]
    u_cf = _dot(h_all, win_ref[:, PC_CF:PC_NA])
    u_sc = _dot(h_all, win_ref[:, PC_SC:PC_CQ])
    u_mla = _dot(h_main, win_ref[:, PC_CQ:PC_CF])
    u_na = _dot(h_main, win_ref[:, PC_NA:P_PAD])

    row = lax.broadcasted_iota(jnp.int32, (tm + 2 * HALO, 1), 0)
    lo_row = jnp.where(j > 0, 0, HALO)
    hi_row = jnp.where(j < pl.num_programs(1) - 1, tm + 2 * HALO, tm + HALO)
    in_seq = (row >= lo_row) & (row < hi_row)
    ext_sc[...] = jnp.where(in_seq, u_sc[:, 2 * GROUP_W:3 * GROUP_W] * u_sc[:, 0:GROUP_W], 0.0)
    ext_cf[...] = jnp.where(in_seq, u_cf[:, 0:GROUP_W] * jax.nn.sigmoid(u_cf[:, GROUP_W:2 * GROUP_W]), 0.0)

    acc = jnp.zeros((tm, GROUP_W), jnp.float32)
    for k in range(SC_KERNEL):
        off = HALO - SC_KERNEL // 2 + k
        acc = acc + ext_sc[off:off + tm, :] * wsc_ref[k:k + 1, :]
    y_sc = u_sc[HALO:HALO + tm, GROUP_W:2 * GROUP_W] * acc

    base = HALO - CF_KERNEL // 2
    s_len = shifted.shape[1]
    for r in range(1, SUBLANES):
        shifted[r] = ext_cf[r:r + s_len, :]
    for c0 in range(0, tm, CONV_ROWS):
        acc = jnp.zeros((CONV_ROWS, GROUP_W), jnp.float32)
        for k in range(CF_KERNEL):
            r, lo = (base + k) % SUBLANES, (base + k) // SUBLANES * SUBLANES + c0
            src = ext_cf[lo:lo + CONV_ROWS, :] if r == 0 else shifted[r, lo:lo + CONV_ROWS, :]
            acc = acc + src * wdw_ref[k:k + 1, :]
        conv_out[c0:c0 + CONV_ROWS, :] = acc
    y = conv_out[...] + bdw_ref[...]
    mu = jnp.mean(y, axis=-1, keepdims=True)
    yc = y - mu
    var = jnp.mean(yc * yc, axis=-1, keepdims=True)
    z = (yc * lax.rsqrt(var + EPS)) * lng_ref[...] + lnb_ref[...]
    z = z * jax.nn.sigmoid(z)
    y_cf = _dot(z.astype(jnp.bfloat16), wpw_ref[...])
    yloc_ref[0, :, 0:GROUP_W] = y_sc.astype(jnp.bfloat16)
    yloc_ref[0, :, GROUP_W:2 * GROUP_W] = y_cf.astype(jnp.bfloat16)

    nq_ref[0] = u_na[:, 0:GROUP_W].astype(jnp.bfloat16)
    nk_ref[0] = u_na[:, GROUP_W:2 * GROUP_W].astype(jnp.bfloat16)
    nv_ref[0] = u_na[:, 2 * GROUP_W:3 * GROUP_W].astype(jnp.bfloat16)

    cqn = (_rms(u_mla[:, 0:MLA_Q_RANK]) * gq_ref[...]).astype(jnp.bfloat16)
    qf = _dot(cqn, wuq_ref[...])
    cq_t = cq_ref[...]
    sq_t = sq_ref[...]
    for hd in range(N_HEADS):
        lo = hd * HEAD_PAD
        qh = qf[:, lo:lo + HEAD_PAD] * cq_t + qf[:, MLA_PAD + lo:MLA_PAD + lo + HEAD_PAD] * sq_t
        q_ref[0, :, lo:lo + HEAD_PAD] = qh.astype(jnp.bfloat16)

    ckv_lo = MLA_Q_RANK
    rope_lo = MLA_Q_RANK + MLA_KV_RANK
    ckvn = (_rms(u_mla[:, ckv_lo:rope_lo]) * gkv_ref[...]).astype(jnp.bfloat16)
    kvf = _dot(ckvn, wukv_ref[...])
    g = u_mla[:, rope_lo:rope_lo + LANES] * tk_ref[...]
    lane = lax.broadcasted_iota(jnp.int32, g.shape, 1)
    rot = pltpu.roll(g, 2 * MLA_ROPE, axis=1) + g
    kr = jnp.where((lane >= MLA_NOPE) & (lane < MLA_NOPE + MLA_ROPE), rot, 0.0)
    for hd in range(N_HEADS):
        lo = hd * HEAD_PAD
        k_ref[0, :, lo:lo + HEAD_PAD] = (kvf[:, lo:lo + HEAD_PAD] + kr).astype(jnp.bfloat16)
        vh = kvf[:, MLA_PAD + lo:MLA_PAD + lo + HEAD_PAD]
        v_ref[0, :, lo:lo + HEAD_PAD] = jnp.where(lane == MLA_V, 1.0, vh).astype(jnp.bfloat16)


def _inproj(x, mods, mod_row, wts, layer, tabs, tm):
    b, l, d = x.shape
    grid = (b, l // tm)
    hb = tm // HALO
    nhb = l // HALO
    tok = lambda w: pl.BlockSpec((1, tm, w), lambda i, j: (i, j, 0))
    prev = pl.BlockSpec((1, HALO, d), lambda i, j: (i, jnp.maximum(j * hb - 1, 0), 0))
    nxt = pl.BlockSpec((1, HALO, d), lambda i, j: (i, jnp.minimum((j + 1) * hb, nhb - 1), 0))
    tab = pl.BlockSpec((tm, LANES), lambda i, j: (j, 0))
    f32, bf16 = jnp.float32, jnp.bfloat16
    widths = [2 * GROUP_W, MLA_PAD, MLA_PAD, MLA_PAD, GROUP_W, GROUP_W, GROUP_W]
    names = ["g_pre_mix", "w_in", "g_q", "w_uq", "g_kv", "w_ukv"]
    conv_names = ["w_sc", "cf_w_dw", "cf_b_dw", "cf_ln_g", "cf_ln_b", "cf_w_pw"]
    s_len = (HALO - CF_KERNEL // 2 + CF_KERNEL - 1) // SUBLANES * SUBLANES + tm
    return pl.pallas_call(
        functools.partial(_inproj_kernel, tm=tm),
        grid=grid,
        in_specs=[tok(d), prev, nxt, _mod_spec(mods, layer, mod_row)]
                 + [_layer_spec(wts[n], layer) for n in names] + [tab, tab, tab]
                 + [_layer_spec(wts[n], layer) for n in conv_names],
        out_specs=[tok(w) for w in widths],
        out_shape=[jax.ShapeDtypeStruct((b, l, w), bf16) for w in widths],
        scratch_shapes=[pltpu.VMEM((tm + 2 * HALO, d), bf16),
                        pltpu.VMEM((tm + 2 * HALO, GROUP_W), f32),
                        pltpu.VMEM((tm + 2 * HALO, GROUP_W), f32),
                        pltpu.VMEM((SUBLANES, s_len, GROUP_W), f32),
                        pltpu.VMEM((tm, GROUP_W), f32)],
        compiler_params=_cparams(2), name="inproj",
    )(x, x, x, mods, *[wts[n] for n in names], tabs["cq"], tabs["sq"], tabs["tk"],
      *[wts[n] for n in conv_names])


def _mla_kernel(*refs, has_lat):
    if has_lat:
        q_ref, kl_ref, vl_ref, kc_ref, vc_ref, o_ref = refs
    else:
        q_ref, kc_ref, vc_ref, o_ref = refs
    chunks = [(kc_ref, vc_ref, 0, kc_ref.shape[1])]
    if has_lat:
        n_lat = kl_ref.shape[1]
        chunks += [(kl_ref, vl_ref, lo, MLA_KCHUNK) for lo in range(0, n_lat, MLA_KCHUNK)]
    c2 = MLA_SCALE * LOG2E
    outs = []
    for hd in range(MLA_HEADS_PER_STEP):
        lanes = slice(hd * HEAD_PAD, (hd + 1) * HEAD_PAD)
        q = q_ref[0, :, lanes]
        m = o = None
        for k_ref, v_ref, lo, n in chunks:
            s = _dot_nt(q, k_ref[0, lo:lo + n, lanes])
            m_chunk = jnp.max(s, axis=-1, keepdims=True)
            m_new = m_chunk if m is None else jnp.maximum(m, m_chunk)
            p = jnp.exp2((s - m_new) * c2).astype(jnp.bfloat16)
            pv = _dot(p, v_ref[0, lo:lo + n, lanes])
            o = pv if o is None else o * jnp.exp2((m - m_new) * c2) + pv
            m = m_new
        outs.append(o * (1.0 / o[:, MLA_V:MLA_V + 1]))
    lane = lax.broadcasted_iota(jnp.int32, outs[0].shape, 1)
    packed = jnp.where(lane < MLA_V, outs[0], pltpu.roll(outs[1], MLA_V, axis=1))
    o_ref[0] = packed.astype(jnp.bfloat16)


def _mla(q, k_lat, v_lat, k_ctx, v_ctx, tq):
    b, lq, _ = q.shape
    lc = k_ctx.shape[1]
    has_lat = k_lat is not None
    hw = MLA_HEADS_PER_STEP * HEAD_PAD
    grid = (b, N_HEADS // MLA_HEADS_PER_STEP, lq // tq)
    qspec = pl.BlockSpec((1, tq, hw), lambda i, h, j: (i, j, h))
    kvspec = lambda n: pl.BlockSpec((1, n, hw), lambda i, h, j: (i, 0, h))
    in_specs = [qspec]
    args = [q]
    if has_lat:
        in_specs += [kvspec(k_lat.shape[1])] * 2
        args += [k_lat, v_lat]
    in_specs += [kvspec(lc)] * 2
    args += [k_ctx, v_ctx]
    return pl.pallas_call(
        functools.partial(_mla_kernel, has_lat=has_lat),
        grid=grid,
        in_specs=in_specs,
        out_specs=pl.BlockSpec((1, tq, MLA_HEADS_PER_STEP * MLA_V), lambda i, h, j: (i, j, h)),
        out_shape=jax.ShapeDtypeStruct((b, lq, N_HEADS * MLA_V), jnp.bfloat16),
        compiler_params=_cparams(3), name="mla_attn",
    )(*args)


def _na_kernel(*refs, has_local):
    if has_local:
        q_ref, k_ref, v_ref, kc_ref, vc_ref, delta_ref, o_ref, bias_ref, planes_ref = refs

        @pl.when((pl.program_id(0) == 0) & (pl.program_id(1) == 0))
        def _():
            w = GRID_W
            lane = lax.broadcasted_iota(jnp.int32, (w, 2 * w), 1)
            left = lane < w
            qc = lax.broadcasted_iota(jnp.int32, (w, 2 * w), 0)
            kc = lane & (w - 1)
            cstart = jnp.clip(qc - NA_COLS // 2, 0, w - NA_COLS)
            in_cols = (kc >= cstart) & (kc < cstart + NA_COLS)
            for hd, o in np.ndindex(N_HEADS, NA_MASKED_PLANE):
                vec = jnp.broadcast_to(delta_ref[hd, o:o + 1, :], (w, 2 * w))
                lo_half = pltpu.roll(vec, w + 1, axis=1, stride=1, stride_axis=0)
                hi_half = pltpu.roll(vec, 1, axis=1, stride=1, stride_axis=0)
                planes_ref[hd, o] = jnp.where(in_cols, jnp.where(left, lo_half, hi_half), MASK_VALUE)
            for hd in range(N_HEADS):
                planes_ref[hd, NA_MASKED_PLANE] = jnp.full((w, 2 * w), MASK_VALUE, jnp.float32)
            for c, qi, kp in np.ndindex(3, NA_QROWS, NA_WROWS // 2):
                pa, pb = (int(v) for v in _NA_PLANE[c, qi, 2 * kp:2 * kp + 2])
                for hd in range(N_HEADS):
                    blk2 = planes_ref[hd, pa] if pa == pb else jnp.where(left, planes_ref[hd, pa],
                                                                         planes_ref[hd, pb])
                    bias_ref[c, hd, qi * GRID_W:(qi + 1) * GRID_W,
                             2 * kp * GRID_W:2 * (kp + 1) * GRID_W] = blk2

        blk = pl.program_id(1)
        nblk = pl.num_programs(1)
        wrow = jnp.clip(blk * NA_QROWS - NA_ROWS // 2, 0, GRID_W - NA_WROWS)
        start = pl.multiple_of(wrow * GRID_W, GRID_W)
        case = jnp.where(blk == 0, 0, jnp.where(blk == nblk - 1, 2, 1))
        kwin = k_ref[0, pl.ds(start, NA_TK), :]
        vwin = v_ref[0, pl.ds(start, NA_TK), :]
    else:
        q_ref, kc_ref, vc_ref, o_ref = refs
    q = q_ref[0]
    kc = kc_ref[0]
    vc = vc_ref[0]
    lane = lax.broadcasted_iota(jnp.int32, q.shape, 1)
    out = jnp.zeros(q.shape, jnp.float32)
    c2 = NA_SCALE * LOG2E
    for hd in range(N_HEADS):
        in_head = (lane >= hd * HEAD_DIM) & (lane < (hd + 1) * HEAD_DIM)
        qm = jnp.where(in_head, q, jnp.zeros_like(q))
        s_c = _dot_nt(qm, kc) * c2
        m = jnp.max(s_c, axis=-1, keepdims=True)
        if has_local:
            s_l = _dot_nt(qm, kwin) * c2 + bias_ref[case, hd]
            m = jnp.maximum(m, jnp.max(s_l, axis=-1, keepdims=True))
        p_c = jnp.exp2(s_c - m)
        den = jnp.sum(p_c, axis=-1, keepdims=True)
        o = _dot(p_c.astype(jnp.bfloat16), vc)
        if has_local:
            p_l = jnp.exp2(s_l - m)
            den = den + jnp.sum(p_l, axis=-1, keepdims=True)
            o = o + _dot(p_l.astype(jnp.bfloat16), vwin)
        out = jnp.where(in_head, o * (1.0 / den), out)
    o_ref[0] = out.astype(jnp.bfloat16)


def _na(q, k, v, k_ctx, v_ctx, planes, layer):
    b, l, w = q.shape
    lc = k_ctx.shape[1]
    has_local = k is not None
    tq = NA_TQ
    grid = (b, l // tq)
    qspec = pl.BlockSpec((1, tq, w), lambda i, j: (i, j, 0))
    full = lambda n: pl.BlockSpec((1, n, w), lambda i, j: (i, 0, 0))
    in_specs = [qspec]
    args = [q]
    if has_local:
        in_specs += [full(l), full(l)]
        args += [k, v]
    in_specs += [full(lc), full(lc)]
    args += [k_ctx, v_ctx]
    scratch = []
    if has_local:
        in_specs += [_layer_spec(planes, layer)]
        args += [planes]
        scratch = [pltpu.VMEM((3, N_HEADS, NA_TQ, NA_TK), jnp.float32),
                   pltpu.VMEM((N_HEADS, NA_MASKED_PLANE + 1, GRID_W, 2 * GRID_W), jnp.float32)]
    return pl.pallas_call(
        functools.partial(_na_kernel, has_local=has_local),
        grid=grid,
        in_specs=in_specs,
        out_specs=qspec,
        out_shape=jax.ShapeDtypeStruct((b, l, w), jnp.bfloat16),
        scratch_shapes=scratch,
        compiler_params=_cparams(2), name="na_attn",
    )(*args)


NA_MASKED_PLANE = 2 * NA_ROWS - 1


def _na_plane_index():
    rows = GRID_W
    idx = np.full((3, NA_QROWS, NA_WROWS), NA_MASKED_PLANE, np.int32)
    for c, r0 in enumerate((0, NA_QROWS, rows - NA_QROWS)):
        wrow = min(max(r0 - NA_ROWS // 2, 0), rows - NA_WROWS)
        for qi in range(NA_QROWS):
            r = r0 + qi
            band = min(max(r - NA_ROWS // 2, 0), rows - NA_ROWS)
            for kj in range(NA_WROWS):
                kr = wrow + kj
                if band <= kr < band + NA_ROWS:
                    idx[c, qi, kj] = kr - r + (NA_ROWS - 1)
    return idx


_NA_PLANE = _na_plane_index()


def _na_delta_table(rpb):
    w = GRID_W
    delta = np.arange(2 * w) - (w - 1)
    return rpb[..., np.clip(delta + (NA_COLS - 1), 0, 2 * NA_COLS - 2)] * LOG2E


def _outmlp_kernel(x_ref, yloc_ref, ymla_ref, yna_ref, mod_ref, gpm_ref, gpre_ref, gpost_ref,
                   wout_ref, w1_ref, w2_ref, o_ref):
    x = x_ref[0]
    ga1 = mod_ref[2:3, :]
    sh2 = mod_ref[3:4, :]
    sc2 = mod_ref[4:5, :]
    ga2 = mod_ref[5:6, :]
    ycat = jnp.concatenate([yloc_ref[0, :, 0:GROUP_W], ymla_ref[0], yloc_ref[0, :, GROUP_W:2 * GROUP_W],
                            yna_ref[0]], axis=-1)
    y = _dot(ycat, wout_ref[...])
    x1 = x + _rms(y) * (gpm_ref[...] * ga1)
    hm = (_rms(x1) * (gpre_ref[...] * (1.0 + sc2)) + sh2).astype(jnp.bfloat16)
    acc = jnp.zeros(x.shape, jnp.float32)
    for c in range(MLP_HIDDEN // MLP_CHUNK):
        lo = c * MLP_CHUNK
        hj = jnp.maximum(_dot(hm, w1_ref[:, lo:lo + MLP_CHUNK]), 0.0)
        acc = acc + _dot((hj * hj).astype(jnp.bfloat16), w2_ref[lo:lo + MLP_CHUNK, :])
    o_ref[0] = x1 + _rms(acc) * (gpost_ref[...] * ga2)


def _outmlp(x, yloc, ymla, yna, mods, mod_row, wts, layer, tm):
    b, l, d = x.shape
    grid = (b, l // tm)
    tok = lambda w: pl.BlockSpec((1, tm, w), lambda i, j: (i, j, 0))
    gains = ["g_post_mix", "g_pre_mlp", "g_post_mlp"]
    mats = ["w_out", "w_mlp1", "w_mlp2"]
    return pl.pallas_call(
        _outmlp_kernel,
        grid=grid,
        in_specs=[tok(d), tok(2 * GROUP_W), tok(GROUP_W), tok(GROUP_W), _mod_spec(mods, layer, mod_row)]
                 + [_layer_spec(wts[n], layer) for n in gains]
                 + [_layer_spec(wts[n], layer, single_buffer=True) for n in mats],
        out_specs=tok(d),
        out_shape=jax.ShapeDtypeStruct((b, l, d), jnp.float32),
        compiler_params=_cparams(2), name="outproj_mlp",
    )(x, yloc, ymla, yna, mods, *[wts[n] for n in gains], *[wts[n] for n in mats])


def _swap_halves_signed(w):
    q = MLA_ROPE // 4
    x1, x2, x3, x4 = (w[..., i * q:(i + 1) * q] for i in range(4))
    return jnp.concatenate([-x2, x1, -x4, x3], axis=-1)


def _prep_weights(p):
    bf16 = jnp.bfloat16
    w_in = p["w_in"]
    depth, d, _ = w_in.shape
    k_rope = w_in[..., MLA_Q_END + MLA_KV_RANK:MLA_KV_END]
    z32 = jnp.zeros((depth, d, MLA_ROPE), w_in.dtype)
    w_in_p = jnp.concatenate([
        w_in[..., :MLA_Q_END + MLA_KV_RANK],
        k_rope, z32, _swap_halves_signed(k_rope), z32,
        w_in[..., MLA_KV_END:]], axis=-1).astype(bf16)

    wq = p["w_uq"].reshape(depth, MLA_Q_RANK, N_HEADS, MLA_NOPE + MLA_ROPE)
    zq = jnp.zeros((depth, MLA_Q_RANK, N_HEADS, HEAD_PAD - MLA_NOPE - MLA_ROPE), wq.dtype)
    zn = jnp.zeros((depth, MLA_Q_RANK, N_HEADS, MLA_NOPE), wq.dtype)
    q_main = jnp.concatenate([wq, zq], axis=-1).reshape(depth, MLA_Q_RANK, MLA_PAD)
    q_swap = jnp.concatenate([zn, _swap_halves_signed(wq[..., MLA_NOPE:]), zq], axis=-1)
    w_uq_p = jnp.concatenate([q_main, q_swap.reshape(depth, MLA_Q_RANK, MLA_PAD)], axis=-1).astype(bf16)

    wkv = p["w_ukv"].reshape(depth, MLA_KV_RANK, N_HEADS, MLA_NOPE + MLA_V)
    zk = jnp.zeros((depth, MLA_KV_RANK, N_HEADS, HEAD_PAD - MLA_NOPE), wkv.dtype)
    k_main = jnp.concatenate([wkv[..., :MLA_NOPE], zk], axis=-1).reshape(depth, MLA_KV_RANK, MLA_PAD)
    v_main = jnp.concatenate([wkv[..., MLA_NOPE:], zk], axis=-1).reshape(depth, MLA_KV_RANK, MLA_PAD)
    w_ukv_p = jnp.concatenate([k_main, v_main], axis=-1).astype(bf16)

    row = lambda a: a.reshape(depth, 1, -1)
    return {
        "g_pre_mix": row(p["g_pre_mix"]), "w_in": w_in_p, "g_q": row(p["g_q"]), "w_uq": w_uq_p,
        "g_kv": row(p["g_kv"]), "w_ukv": w_ukv_p, "w_sc": p["w_sc"], "cf_w_dw": p["cf_w_dw"],
        "cf_b_dw": row(p["cf_b_dw"]), "cf_ln_g": row(p["cf_ln_g"]), "cf_ln_b": row(p["cf_ln_b"]),
        "cf_w_pw": p["cf_w_pw"].astype(bf16), "w_out": p["w_out"].astype(bf16), "g_post_mix": row(p["g_post_mix"]),
        "g_pre_mlp": row(p["g_pre_mlp"]), "w_mlp1": p["w_mlp1"].astype(bf16),
        "w_mlp2": p["w_mlp2"].astype(bf16), "g_post_mlp": row(p["g_post_mlp"]),
        "na_delta": _na_delta_table(p["na_rpb"]),
    }


def _rope_tables(l, rotate):
    f32 = np.float32
    ones = np.ones((l, MLA_NOPE), f32)
    zpad = np.zeros((l, HEAD_PAD - MLA_NOPE - MLA_ROPE), f32)
    z32 = np.zeros((l, MLA_ROPE), f32)
    if rotate:
        pos = np.arange(l)
        n_freq = MLA_ROPE // 4
        inv_freq = (f32(ROPE_BASE) ** (-np.arange(n_freq, dtype=f32) / f32(n_freq))).astype(f32)
        ang_row = (pos // GRID_W).astype(f32)[:, None] * inv_freq
        ang_col = (pos % GRID_W).astype(f32)[:, None] * inv_freq
        cos = np.concatenate([np.cos(ang_row)] * 2 + [np.cos(ang_col)] * 2, axis=-1).astype(f32)
        sin = np.concatenate([np.sin(ang_row)] * 2 + [np.sin(ang_col)] * 2, axis=-1).astype(f32)
    else:
        cos = np.ones((l, MLA_ROPE), f32)
        sin = z32
    return {"cq": jnp.asarray(np.concatenate([ones, cos, zpad], axis=-1)),
            "sq": jnp.asarray(np.concatenate([np.zeros_like(ones), sin, zpad], axis=-1)),
            "tk": jnp.asarray(np.concatenate([cos, z32, sin, z32], axis=-1))}


def kernel(x, c, ctx, c_ctx, w_mod, b_mod, g_pre_mix, w_in, w_sc, g_q, w_uq, g_kv, w_ukv, cf_w_dw,
           cf_b_dw, cf_ln_g, cf_ln_b, cf_w_pw, na_rpb, w_out, g_post_mix, g_pre_mlp, w_mlp1, w_mlp2,
           g_post_mlp):
    b, l, d = x.shape
    lc = ctx.shape[1]
    depth = w_mod.shape[0]
    wts = _prep_weights({
        "g_pre_mix": g_pre_mix, "w_in": w_in, "w_sc": w_sc, "g_q": g_q, "w_uq": w_uq, "g_kv": g_kv,
        "w_ukv": w_ukv, "cf_w_dw": cf_w_dw, "cf_b_dw": cf_b_dw, "cf_ln_g": cf_ln_g, "cf_ln_b": cf_ln_b,
        "cf_w_pw": cf_w_pw, "na_rpb": na_rpb, "w_out": w_out, "g_post_mix": g_post_mix,
        "g_pre_mlp": g_pre_mlp, "w_mlp1": w_mlp1, "w_mlp2": w_mlp2, "g_post_mlp": g_post_mlp})

    n_rows = 8
    cs = jnp.concatenate([c, c_ctx[None, :], jnp.zeros((n_rows - b - 1, d), c.dtype)], axis=0)
    mods = _modulation(cs, w_mod, b_mod).reshape(depth, n_rows, N_MOD, d)
    ctx_row = b
    tabs_lat = _rope_tables(l, True)
    tabs_ctx = _rope_tables(lc, False)

    tm = 1024
    tq = 1024
    xc = ctx
    for i in range(depth):
        last = i == depth - 1
        y_loc, q, k, v, nq, nk, nv = _inproj(x, mods, None, wts, i, tabs_lat, tm)
        yc_loc, q_c, k_c, v_c, nq_c, nk_c, nv_c = _inproj(xc, mods, ctx_row, wts, i, tabs_ctx, lc)

        y_mla = _mla(q, k, v, k_c, v_c, tq)
        y_na = _na(nq, nk, nv, nk_c, nv_c, wts["na_delta"], i)
        x = _outmlp(x, y_loc, y_mla, y_na, mods, None, wts, i, tm)

        if not last:
            yc_mla = _mla(q_c, None, None, k_c, v_c, lc)
            yc_na = _na(nq_c, None, None, nk_c, nv_c, None, i)
            xc = _outmlp(xc, yc_loc, yc_mla, yc_na, mods, ctx_row, wts, i, lc)
    return x
```

```python
import functools

import jax
import jax.numpy as jnp
import numpy as np
from jax import lax
from jax.experimental import pallas as pl
from jax.experimental.pallas import tpu as pltpu

D_MODEL = 1024
GRID_W = 64
GROUP_W = D_MODEL // 4
HEAD_DIM = 64
N_HEADS = GROUP_W // HEAD_DIM
SC_KERNEL = 3
MLA_Q_RANK = 256
MLA_KV_RANK = 128
MLA_NOPE = 64
MLA_ROPE = 32
MLA_V = 64
MLA_SCALE = (MLA_NOPE + MLA_ROPE) ** -0.5
CF_KERNEL = 31
NA_ROWS = 8
NA_COLS = 16
NA_SCALE = HEAD_DIM ** -0.5
ROPE_BASE = 10000.0
MLP_HIDDEN = 4 * D_MODEL
N_MOD = 6
EPS = 1e-6

SC_END = 3 * GROUP_W
MLA_Q_END = SC_END + MLA_Q_RANK
MLA_KV_END = MLA_Q_END + MLA_KV_RANK + MLA_ROPE
CF_END = MLA_KV_END + 2 * GROUP_W
NA_Q_END = CF_END + GROUP_W
P_IN = NA_Q_END + 2 * GROUP_W

LANES = 128
SUBLANES = 8
HEAD_PAD = LANES
MLA_PAD = N_HEADS * HEAD_PAD

NA_QROWS = 4
NA_WROWS = NA_QROWS + NA_ROWS
NA_TQ = NA_QROWS * GRID_W
NA_TK = NA_WROWS * GRID_W
HALO = 16
CONV_ROWS = 64
MASK_VALUE = -1e30
MLA_KCHUNK = 2048
MLA_HEADS_PER_STEP = 4
LOG2E = 1.4426950408889634
MLP_CHUNK = 1024

VMEM_LIMIT = 56 * 1024 * 1024


def _cparams(n_axes):
    return pltpu.CompilerParams(dimension_semantics=("arbitrary",) * n_axes,
                                vmem_limit_bytes=VMEM_LIMIT)


def _layer_spec(arr, layer, single_buffer=False):
    nd = arr.ndim - 1
    kwargs = {"pipeline_mode": pl.Buffered(1)} if single_buffer else {}
    return pl.BlockSpec((None,) + arr.shape[1:], lambda *_: (layer,) + (0,) * nd, **kwargs)


def _mod_spec(mods, layer, row):
    if row is None:
        return pl.BlockSpec((None, None) + mods.shape[2:], lambda i, j: (layer, i, 0, 0))
    return pl.BlockSpec((None, None) + mods.shape[2:], lambda i, j: (layer, row, 0, 0))


def _rms(x):
    return x * lax.rsqrt(jnp.mean(x * x, axis=-1, keepdims=True) + EPS)


def _dot(a, b):
    return jnp.dot(a, b, preferred_element_type=jnp.float32)


def _dot_nt(a, b):
    return lax.dot_general(a, b, (((1,), (1,)), ((), ())), preferred_element_type=jnp.float32)


def _mod_kernel(cs_ref, w_ref, b_ref, o_ref):
    cs = cs_ref[...]
    a = (cs * jax.nn.sigmoid(cs)).astype(jnp.bfloat16)
    o_ref[0] = _dot(a, w_ref[0].astype(jnp.bfloat16)) + b_ref[0]


def _modulation(cs, w_mod, b_mod):
    depth, d, n = w_mod.shape
    bn = 1024
    return pl.pallas_call(
        _mod_kernel,
        grid=(depth, n // bn),
        in_specs=[pl.BlockSpec(cs.shape, lambda l, j: (0, 0)),
                  pl.BlockSpec((1, d, bn), lambda l, j: (l, 0, j)),
                  pl.BlockSpec((1, 1, bn), lambda l, j: (l, 0, j))],
        out_specs=pl.BlockSpec((1, cs.shape[0], bn), lambda l, j: (l, 0, j)),
        out_shape=jax.ShapeDtypeStruct((depth, cs.shape[0], n), jnp.float32),
        compiler_params=_cparams(2), name="modulation",
    )(cs, w_mod, b_mod.reshape(depth, 1, n))


def _inproj_kernel(x_ref, xp_ref, xn_ref, mod_ref, gpre_ref, wsc_in_ref, wmla_in_ref, wcf_in_ref, wna_in_ref,
                   gq_ref, wuq_ref, gkv_ref, wukv_ref,
                   cq_ref, sq_ref, tk_ref, wsc_ref, wdw_ref, bdw_ref, lng_ref, lnb_ref, wpw_ref,
                   yloc_ref, q_ref, k_ref, v_ref, nq_ref, nk_ref, nv_ref,
                   h_ext, ext_sc, ext_cf, shifted, conv_out, *, tm):
    j = pl.program_id(1)
    sh1 = mod_ref[0:1, :]
    sc1 = mod_ref[1:2, :]

    gain = gpre_ref[...] * (1.0 + sc1)

    def modulated(x):
        return (_rms(x) * gain + sh1).astype(jnp.bfloat16)

    h_ext[0:HALO, :] = modulated(xp_ref[0])
    h_ext[HALO:HALO + tm, :] = modulated(x_ref[0])
    h_ext[HALO + tm:2 * HALO + tm, :] = modulated(xn_ref[0])
    h_all = h_ext[...]
    h_main = h_ext[HALO:HALO + tm, :]
    u_cf = _dot(h_all, wcf_in_ref[...])
    u_sc = _dot(h_all, wsc_in_ref[...])
    u_mla = _dot(h_main, wmla_in_ref[...])
    u_na = _dot(h_main, wna_in_ref[...])

    row = lax.broadcasted_iota(jnp.int32, (tm + 2 * HALO, 1), 0)
    lo_row = jnp.where(j > 0, 0, HALO)
    hi_row = jnp.where(j < pl.num_programs(1) - 1, tm + 2 * HALO, tm + HALO)
    in_seq = (row >= lo_row) & (row < hi_row)
    ext_sc[...] = jnp.where(in_seq, u_sc[:, 2 * GROUP_W:3 * GROUP_W] * u_sc[:, 0:GROUP_W], 0.0)
    ext_cf[...] = jnp.where(in_seq, u_cf[:, 0:GROUP_W] * jax.nn.sigmoid(u_cf[:, GROUP_W:2 * GROUP_W]), 0.0)

    acc = jnp.zeros((tm, GROUP_W), jnp.float32)
    for k in range(SC_KERNEL):
        off = HALO - SC_KERNEL // 2 + k
        acc = acc + ext_sc[off:off + tm, :] * wsc_ref[k:k + 1, :]
    y_sc = u_sc[HALO:HALO + tm, GROUP_W:2 * GROUP_W] * acc

    base = HALO - CF_KERNEL // 2
    s_len = shifted.shape[1]
    for r in range(1, SUBLANES):
        shifted[r] = ext_cf[r:r + s_len, :]
    for c0 in range(0, tm, CONV_ROWS):
        acc = jnp.zeros((CONV_ROWS, GROUP_W), jnp.float32)
        for k in range(CF_KERNEL):
            r, lo = (base + k) % SUBLANES, (base + k) // SUBLANES * SUBLANES + c0
            src = ext_cf[lo:lo + CONV_ROWS, :] if r == 0 else shifted[r, lo:lo + CONV_ROWS, :]
            acc = acc + src * wdw_ref[k:k + 1, :]
        conv_out[c0:c0 + CONV_ROWS, :] = acc
    y = conv_out[...] + bdw_ref[...]
    mu = jnp.mean(y, axis=-1, keepdims=True)
    yc = y - mu
    var = jnp.mean(yc * yc, axis=-1, keepdims=True)
    z = (yc * lax.rsqrt(var + EPS)) * lng_ref[...] + lnb_ref[...]
    z = z * jax.nn.sigmoid(z)
    y_cf = _dot(z.astype(jnp.bfloat16), wpw_ref[...])
    yloc_ref[0, :, 0:GROUP_W] = y_sc.astype(jnp.bfloat16)
    yloc_ref[0, :, GROUP_W:2 * GROUP_W] = y_cf.astype(jnp.bfloat16)

    nq_ref[0] = u_na[:, 0:GROUP_W].astype(jnp.bfloat16)
    nk_ref[0] = u_na[:, GROUP_W:2 * GROUP_W].astype(jnp.bfloat16)
    nv_ref[0] = u_na[:, 2 * GROUP_W:3 * GROUP_W].astype(jnp.bfloat16)

    cqn = (_rms(u_mla[:, 0:MLA_Q_RANK]) * gq_ref[...]).astype(jnp.bfloat16)
    qf = _dot(cqn, wuq_ref[...])
    cq_t = cq_ref[...]
    sq_t = sq_ref[...]
    for hd in range(N_HEADS):
        lo = hd * HEAD_PAD
        qh = qf[:, lo:lo + HEAD_PAD] * cq_t + qf[:, MLA_PAD + lo:MLA_PAD + lo + HEAD_PAD] * sq_t
        q_ref[0, :, lo:lo + HEAD_PAD] = qh.astype(jnp.bfloat16)

    ckv_lo = MLA_Q_RANK
    rope_lo = MLA_Q_RANK + MLA_KV_RANK
    ckvn = (_rms(u_mla[:, ckv_lo:rope_lo]) * gkv_ref[...]).astype(jnp.bfloat16)
    kvf = _dot(ckvn, wukv_ref[...])
    g = u_mla[:, rope_lo:rope_lo + LANES] * tk_ref[...]
    lane = lax.broadcasted_iota(jnp.int32, g.shape, 1)
    rot = pltpu.roll(g, 2 * MLA_ROPE, axis=1) + g
    kr = jnp.where((lane >= MLA_NOPE) & (lane < MLA_NOPE + MLA_ROPE), rot, 0.0)
    for hd in range(N_HEADS):
        lo = hd * HEAD_PAD
        k_ref[0, :, lo:lo + HEAD_PAD] = (kvf[:, lo:lo + HEAD_PAD] + kr).astype(jnp.bfloat16)
        vh = kvf[:, MLA_PAD + lo:MLA_PAD + lo + HEAD_PAD]
        v_ref[0, :, lo:lo + HEAD_PAD] = jnp.where(lane == MLA_V, 1.0, vh).astype(jnp.bfloat16)


def _inproj(x, mods, mod_row, wts, layer, tabs, tm):
    b, l, d = x.shape
    grid = (b, l // tm)
    hb = tm // HALO
    nhb = l // HALO
    tok = lambda w: pl.BlockSpec((1, tm, w), lambda i, j: (i, j, 0))
    prev = pl.BlockSpec((1, HALO, d), lambda i, j: (i, jnp.maximum(j * hb - 1, 0), 0))
    nxt = pl.BlockSpec((1, HALO, d), lambda i, j: (i, jnp.minimum((j + 1) * hb, nhb - 1), 0))
    tab = pl.BlockSpec((tm, LANES), lambda i, j: (j, 0))
    f32, bf16 = jnp.float32, jnp.bfloat16
    widths = [2 * GROUP_W, MLA_PAD, MLA_PAD, MLA_PAD, GROUP_W, GROUP_W, GROUP_W]
    names = ["g_pre_mix", "w_in_sc", "w_in_mla", "w_in_cf", "w_in_na", "g_q", "w_uq", "g_kv", "w_ukv"]
    conv_names = ["w_sc", "cf_w_dw", "cf_b_dw", "cf_ln_g", "cf_ln_b", "cf_w_pw"]
    s_len = (HALO - CF_KERNEL // 2 + CF_KERNEL - 1) // SUBLANES * SUBLANES + tm
    return pl.pallas_call(
        functools.partial(_inproj_kernel, tm=tm),
        grid=grid,
        in_specs=[tok(d), prev, nxt, _mod_spec(mods, layer, mod_row)]
                 + [_layer_spec(wts[n], layer) for n in names] + [tab, tab, tab]
                 + [_layer_spec(wts[n], layer) for n in conv_names],
        out_specs=[tok(w) for w in widths],
        out_shape=[jax.ShapeDtypeStruct((b, l, w), bf16) for w in widths],
        scratch_shapes=[pltpu.VMEM((tm + 2 * HALO, d), bf16),
                        pltpu.VMEM((tm + 2 * HALO, GROUP_W), f32),
                        pltpu.VMEM((tm + 2 * HALO, GROUP_W), f32),
                        pltpu.VMEM((SUBLANES, s_len, GROUP_W), f32),
                        pltpu.VMEM((tm, GROUP_W), f32)],
        compiler_params=_cparams(2), name="inproj",
    )(x, x, x, mods, *[wts[n] for n in names], tabs["cq"], tabs["sq"], tabs["tk"],
      *[wts[n] for n in conv_names])


def _mla_kernel(*refs, has_lat):
    if has_lat:
        q_ref, kl_ref, vl_ref, kc_ref, vc_ref, o_ref = refs
    else:
        q_ref, kc_ref, vc_ref, o_ref = refs
    chunks = [(kc_ref, vc_ref, 0, kc_ref.shape[1])]
    if has_lat:
        n_lat = kl_ref.shape[1]
        chunks += [(kl_ref, vl_ref, lo, MLA_KCHUNK) for lo in range(0, n_lat, MLA_KCHUNK)]
    c2 = MLA_SCALE * LOG2E
    outs = []
    for hd in range(MLA_HEADS_PER_STEP):
        lanes = slice(hd * HEAD_PAD, (hd + 1) * HEAD_PAD)
        q = q_ref[0, :, lanes]
        m = o = None
        for k_ref, v_ref, lo, n in chunks:
            s = _dot_nt(q, k_ref[0, lo:lo + n, lanes])
            m_chunk = jnp.max(s, axis=-1, keepdims=True)
            m_new = m_chunk if m is None else jnp.maximum(m, m_chunk)
            p = jnp.exp2((s - m_new) * c2).astype(jnp.bfloat16)
            pv = _dot(p, v_ref[0, lo:lo + n, lanes])
            o = pv if o is None else o * jnp.exp2((m - m_new) * c2) + pv
            m = m_new
        outs.append(o * (1.0 / o[:, MLA_V:MLA_V + 1]))
    lane = lax.broadcasted_iota(jnp.int32, outs[0].shape, 1)
    for pair in range(MLA_HEADS_PER_STEP // 2):
        packed = jnp.where(lane < MLA_V, outs[2 * pair], pltpu.roll(outs[2 * pair + 1], MLA_V, axis=1))
        o_ref[0, :, pair * HEAD_PAD:(pair + 1) * HEAD_PAD] = packed.astype(jnp.bfloat16)


def _mla(q, k_lat, v_lat, k_ctx, v_ctx, tq):
    b, lq, _ = q.shape
    lc = k_ctx.shape[1]
    has_lat = k_lat is not None
    hw = MLA_HEADS_PER_STEP * HEAD_PAD
    grid = (b, N_HEADS // MLA_HEADS_PER_STEP, lq // tq)
    qspec = pl.BlockSpec((1, tq, hw), lambda i, h, j: (i, j, h))
    kvspec = lambda n: pl.BlockSpec((1, n, hw), lambda i, h, j: (i, 0, h))
    in_specs = [qspec]
    args = [q]
    if has_lat:
        in_specs += [kvspec(k_lat.shape[1])] * 2
        args += [k_lat, v_lat]
    in_specs += [kvspec(lc)] * 2
    args += [k_ctx, v_ctx]
    return pl.pallas_call(
        functools.partial(_mla_kernel, has_lat=has_lat),
        grid=grid,
        in_specs=in_specs,
        out_specs=pl.BlockSpec((1, tq, MLA_HEADS_PER_STEP * MLA_V), lambda i, h, j: (i, j, h)),
        out_shape=jax.ShapeDtypeStruct((b, lq, N_HEADS * MLA_V), jnp.bfloat16),
        compiler_params=_cparams(3), name="mla_attn",
    )(*args)


def _na_kernel(*refs, has_local):
    if has_local:
        q_ref, k_ref, v_ref, kc_ref, vc_ref, delta_ref, o_ref, bias_ref, planes_ref = refs

        @pl.when((pl.program_id(0) == 0) & (pl.program_id(1) == 0))
        def _():
            w = GRID_W
            lane = lax.broadcasted_iota(jnp.int32, (w, 2 * w), 1)
            left = lane < w
            qc = lax.broadcasted_iota(jnp.int32, (w, 2 * w), 0)
            kc = lane & (w - 1)
            cstart = jnp.clip(qc - NA_COLS // 2, 0, w - NA_COLS)
            in_cols = (kc >= cstart) & (kc < cstart + NA_COLS)
            for hd, o in np.ndindex(N_HEADS, NA_MASKED_PLANE):
                vec = jnp.broadcast_to(delta_ref[hd, o:o + 1, :], (w, 2 * w))
                lo_half = pltpu.roll(vec, w + 1, axis=1, stride=1, stride_axis=0)
                hi_half = pltpu.roll(vec, 1, axis=1, stride=1, stride_axis=0)
                planes_ref[hd, o] = jnp.where(in_cols, jnp.where(left, lo_half, hi_half), MASK_VALUE)
            for hd in range(N_HEADS):
                planes_ref[hd, NA_MASKED_PLANE] = jnp.full((w, 2 * w), MASK_VALUE, jnp.float32)
            for c, qi, kp in np.ndindex(3, NA_QROWS, NA_WROWS // 2):
                pa, pb = (int(v) for v in _NA_PLANE[c, qi, 2 * kp:2 * kp + 2])
                for hd in range(N_HEADS):
                    blk2 = planes_ref[hd, pa] if pa == pb else jnp.where(left, planes_ref[hd, pa],
                                                                         planes_ref[hd, pb])
                    bias_ref[c, hd, qi * GRID_W:(qi + 1) * GRID_W,
                             2 * kp * GRID_W:2 * (kp + 1) * GRID_W] = blk2

        blk = pl.program_id(1)
        nblk = pl.num_programs(1)
        wrow = jnp.clip(blk * NA_QROWS - NA_ROWS // 2, 0, GRID_W - NA_WROWS)
        start = pl.multiple_of(wrow * GRID_W, GRID_W)
        case = jnp.where(blk == 0, 0, jnp.where(blk == nblk - 1, 2, 1))
        kwin = k_ref[0, pl.ds(start, NA_TK), :]
        vwin = v_ref[0, pl.ds(start, NA_TK), :]
    else:
        q_ref, kc_ref, vc_ref, o_ref = refs
    q = q_ref[0]
    kc = kc_ref[0]
    vc = vc_ref[0]
    lane = lax.broadcasted_iota(jnp.int32, q.shape, 1)
    out = jnp.zeros(q.shape, jnp.float32)
    c2 = NA_SCALE * LOG2E
    for hd in range(N_HEADS):
        in_head = (lane >= hd * HEAD_DIM) & (lane < (hd + 1) * HEAD_DIM)
        qm = jnp.where(in_head, q, jnp.zeros_like(q))
        s_c = _dot_nt(qm, kc) * c2
        m = jnp.max(s_c, axis=-1, keepdims=True)
        if has_local:
            s_l = _dot_nt(qm, kwin) * c2 + bias_ref[case, hd]
            m = jnp.maximum(m, jnp.max(s_l, axis=-1, keepdims=True))
        p_c = jnp.exp2(s_c - m)
        den = jnp.sum(p_c, axis=-1, keepdims=True)
        o = _dot(p_c.astype(jnp.bfloat16), vc)
        if has_local:
            p_l = jnp.exp2(s_l - m)
            den = den + jnp.sum(p_l, axis=-1, keepdims=True)
            o = o + _dot(p_l.astype(jnp.bfloat16), vwin)
        out = jnp.where(in_head, o * (1.0 / den), out)
    o_ref[0] = out.astype(jnp.bfloat16)


def _na(q, k, v, k_ctx, v_ctx, planes, layer):
    b, l, w = q.shape
    lc = k_ctx.shape[1]
    has_local = k is not None
    tq = NA_TQ
    grid = (b, l // tq)
    qspec = pl.BlockSpec((1, tq, w), lambda i, j: (i, j, 0))
    full = lambda n: pl.BlockSpec((1, n, w), lambda i, j: (i, 0, 0))
    in_specs = [qspec]
    args = [q]
    if has_local:
        in_specs += [full(l), full(l)]
        args += [k, v]
    in_specs += [full(lc), full(lc)]
    args += [k_ctx, v_ctx]
    scratch = []
    if has_local:
        in_specs += [_layer_spec(planes, layer)]
        args += [planes]
        scratch = [pltpu.VMEM((3, N_HEADS, NA_TQ, NA_TK), jnp.float32),
                   pltpu.VMEM((N_HEADS, NA_MASKED_PLANE + 1, GRID_W, 2 * GRID_W), jnp.float32)]
    return pl.pallas_call(
        functools.partial(_na_kernel, has_local=has_local),
        grid=grid,
        in_specs=in_specs,
        out_specs=qspec,
        out_shape=jax.ShapeDtypeStruct((b, l, w), jnp.bfloat16),
        scratch_shapes=scratch,
        compiler_params=_cparams(2), name="na_attn",
    )(*args)


NA_MASKED_PLANE = 2 * NA_ROWS - 1


def _na_plane_index():
    rows = GRID_W
    idx = np.full((3, NA_QROWS, NA_WROWS), NA_MASKED_PLANE, np.int32)
    for c, r0 in enumerate((0, NA_QROWS, rows - NA_QROWS)):
        wrow = min(max(r0 - NA_ROWS // 2, 0), rows - NA_WROWS)
        for qi in range(NA_QROWS):
            r = r0 + qi
            band = min(max(r - NA_ROWS // 2, 0), rows - NA_ROWS)
            for kj in range(NA_WROWS):
                kr = wrow + kj
                if band <= kr < band + NA_ROWS:
                    idx[c, qi, kj] = kr - r + (NA_ROWS - 1)
    return idx


_NA_PLANE = _na_plane_index()


def _na_delta_table(rpb):
    w = GRID_W
    delta = np.arange(2 * w) - (w - 1)
    return rpb[..., np.clip(delta + (NA_COLS - 1), 0, 2 * NA_COLS - 2)] * LOG2E


def _outmlp_kernel(x_ref, yloc_ref, ymla_ref, yna_ref, mod_ref, gpm_ref, gpre_ref, gpost_ref,
                   wout_ref, w1_ref, w2_ref, o_ref):
    x = x_ref[0]
    ga1 = mod_ref[2:3, :]
    sh2 = mod_ref[3:4, :]
    sc2 = mod_ref[4:5, :]
    ga2 = mod_ref[5:6, :]
    ycat = jnp.concatenate([yloc_ref[0, :, 0:GROUP_W], ymla_ref[0], yloc_ref[0, :, GROUP_W:2 * GROUP_W],
                            yna_ref[0]], axis=-1)
    y = _dot(ycat, wout_ref[...])
    x1 = x + _rms(y) * (gpm_ref[...] * ga1)
    hm = (_rms(x1) * (gpre_ref[...] * (1.0 + sc2)) + sh2).astype(jnp.bfloat16)
    acc = jnp.zeros(x.shape, jnp.float32)
    for c in range(MLP_HIDDEN // MLP_CHUNK):
        lo = c * MLP_CHUNK
        hj = jnp.maximum(_dot(hm, w1_ref[:, lo:lo + MLP_CHUNK]), 0.0)
        acc = acc + _dot((hj * hj).astype(jnp.bfloat16), w2_ref[lo:lo + MLP_CHUNK, :])
    o_ref[0] = x1 + _rms(acc) * (gpost_ref[...] * ga2)


def _outmlp(x, yloc, ymla, yna, mods, mod_row, wts, layer, tm):
    b, l, d = x.shape
    grid = (b, l // tm)
    tok = lambda w: pl.BlockSpec((1, tm, w), lambda i, j: (i, j, 0))
    gains = ["g_post_mix", "g_pre_mlp", "g_post_mlp"]
    mats = ["w_out", "w_mlp1", "w_mlp2"]
    return pl.pallas_call(
        _outmlp_kernel,
        grid=grid,
        in_specs=[tok(d), tok(2 * GROUP_W), tok(GROUP_W), tok(GROUP_W), _mod_spec(mods, layer, mod_row)]
                 + [_layer_spec(wts[n], layer) for n in gains]
                 + [_layer_spec(wts[n], layer, single_buffer=True) for n in mats],
        out_specs=tok(d),
        out_shape=jax.ShapeDtypeStruct((b, l, d), jnp.float32),
        compiler_params=_cparams(2), name="outproj_mlp",
    )(x, yloc, ymla, yna, mods, *[wts[n] for n in gains], *[wts[n] for n in mats])


def _swap_halves_signed(w):
    q = MLA_ROPE // 4
    x1, x2, x3, x4 = (w[..., i * q:(i + 1) * q] for i in range(4))
    return jnp.concatenate([-x2, x1, -x4, x3], axis=-1)


def _prep_weights(p):
    bf16 = jnp.bfloat16
    w_in = p["w_in"]
    depth, d, _ = w_in.shape
    k_rope = w_in[..., MLA_Q_END + MLA_KV_RANK:MLA_KV_END]
    z32 = jnp.zeros((depth, d, MLA_ROPE), w_in.dtype)
    w_in_sc = w_in[..., :SC_END].astype(bf16)
    w_in_mla = jnp.concatenate([w_in[..., SC_END:MLA_Q_END + MLA_KV_RANK],
                                k_rope, z32, _swap_halves_signed(k_rope), z32], axis=-1).astype(bf16)
    w_in_cf = w_in[..., MLA_KV_END:CF_END].astype(bf16)
    w_in_na = w_in[..., CF_END:].astype(bf16)

    wq = p["w_uq"].reshape(depth, MLA_Q_RANK, N_HEADS, MLA_NOPE + MLA_ROPE)
    zq = jnp.zeros((depth, MLA_Q_RANK, N_HEADS, HEAD_PAD - MLA_NOPE - MLA_ROPE), wq.dtype)
    zn = jnp.zeros((depth, MLA_Q_RANK, N_HEADS, MLA_NOPE), wq.dtype)
    q_main = jnp.concatenate([wq, zq], axis=-1).reshape(depth, MLA_Q_RANK, MLA_PAD)
    q_swap = jnp.concatenate([zn, _swap_halves_signed(wq[..., MLA_NOPE:]), zq], axis=-1)
    w_uq_p = jnp.concatenate([q_main, q_swap.reshape(depth, MLA_Q_RANK, MLA_PAD)], axis=-1).astype(bf16)

    wkv = p["w_ukv"].reshape(depth, MLA_KV_RANK, N_HEADS, MLA_NOPE + MLA_V)
    zk = jnp.zeros((depth, MLA_KV_RANK, N_HEADS, HEAD_PAD - MLA_NOPE), wkv.dtype)
    k_main = jnp.concatenate([wkv[..., :MLA_NOPE], zk], axis=-1).reshape(depth, MLA_KV_RANK, MLA_PAD)
    v_main = jnp.concatenate([wkv[..., MLA_NOPE:], zk], axis=-1).reshape(depth, MLA_KV_RANK, MLA_PAD)
    w_ukv_p = jnp.concatenate([k_main, v_main], axis=-1).astype(bf16)

    row = lambda a: a.reshape(depth, 1, -1)
    return {
        "g_pre_mix": row(p["g_pre_mix"]), "w_in_sc": w_in_sc, "w_in_mla": w_in_mla, "w_in_cf": w_in_cf,
        "w_in_na": w_in_na, "g_q": row(p["g_q"]), "w_uq": w_uq_p,
        "g_kv": row(p["g_kv"]), "w_ukv": w_ukv_p, "w_sc": p["w_sc"], "cf_w_dw": p["cf_w_dw"],
        "cf_b_dw": row(p["cf_b_dw"]), "cf_ln_g": row(p["cf_ln_g"]), "cf_ln_b": row(p["cf_ln_b"]),
        "cf_w_pw": p["cf_w_pw"].astype(bf16), "w_out": p["w_out"].astype(bf16), "g_post_mix": row(p["g_post_mix"]),
        "g_pre_mlp": row(p["g_pre_mlp"]), "w_mlp1": p["w_mlp1"].astype(bf16),
        "w_mlp2": p["w_mlp2"].astype(bf16), "g_post_mlp": row(p["g_post_mlp"]),
        "na_delta": _na_delta_table(p["na_rpb"]),
    }


def _rope_tables(l, rotate):
    f32 = np.float32
    ones = np.ones((l, MLA_NOPE), f32)
    zpad = np.zeros((l, HEAD_PAD - MLA_NOPE - MLA_ROPE), f32)
    z32 = np.zeros((l, MLA_ROPE), f32)
    if rotate:
        pos = np.arange(l)
        n_freq = MLA_ROPE // 4
        inv_freq = (f32(ROPE_BASE) ** (-np.arange(n_freq, dtype=f32) / f32(n_freq))).astype(f32)
        ang_row = (pos // GRID_W).astype(f32)[:, None] * inv_freq
        ang_col = (pos % GRID_W).astype(f32)[:, None] * inv_freq
        cos = np.concatenate([np.cos(ang_row)] * 2 + [np.cos(ang_col)] * 2, axis=-1).astype(f32)
        sin = np.concatenate([np.sin(ang_row)] * 2 + [np.sin(ang_col)] * 2, axis=-1).astype(f32)
    else:
        cos = np.ones((l, MLA_ROPE), f32)
        sin = z32
    return {"cq": jnp.asarray(np.concatenate([ones, cos, zpad], axis=-1)),
            "sq": jnp.asarray(np.concatenate([np.zeros_like(ones), sin, zpad], axis=-1)),
            "tk": jnp.asarray(np.concatenate([cos, z32, sin, z32], axis=-1))}


def kernel(x, c, ctx, c_ctx, w_mod, b_mod, g_pre_mix, w_in, w_sc, g_q, w_uq, g_kv, w_ukv, cf_w_dw,
           cf_b_dw, cf_ln_g, cf_ln_b, cf_w_pw, na_rpb, w_out, g_post_mix, g_pre_mlp, w_mlp1, w_mlp2,
           g_post_mlp):
    b, l, d = x.shape
    lc = ctx.shape[1]
    depth = w_mod.shape[0]
    wts = _prep_weights({
        "g_pre_mix": g_pre_mix, "w_in": w_in, "w_sc": w_sc, "g_q": g_q, "w_uq": w_uq, "g_kv": g_kv,
        "w_ukv": w_ukv, "cf_w_dw": cf_w_dw, "cf_b_dw": cf_b_dw, "cf_ln_g": cf_ln_g, "cf_ln_b": cf_ln_b,
        "cf_w_pw": cf_w_pw, "na_rpb": na_rpb, "w_out": w_out, "g_post_mix": g_post_mix,
        "g_pre_mlp": g_pre_mlp, "w_mlp1": w_mlp1, "w_mlp2": w_mlp2, "g_post_mlp": g_post_mlp})

    n_rows = 8
    cs = jnp.concatenate([c, c_ctx[None, :], jnp.zeros((n_rows - b - 1, d), c.dtype)], axis=0)
    mods = _modulation(cs, w_mod, b_mod).reshape(depth, n_rows, N_MOD, d)
    ctx_row = b
    tabs_lat = _rope_tables(l, True)
    tabs_ctx = _rope_tables(lc, False)

    tm = 1024
    tq = 1024
    xc = ctx
    for i in range(depth):
        last = i == depth - 1
        y_loc, q, k, v, nq, nk, nv = _inproj(x, mods, None, wts, i, tabs_lat, tm)
        yc_loc, q_c, k_c, v_c, nq_c, nk_c, nv_c = _inproj(xc, mods, ctx_row, wts, i, tabs_ctx, lc)

        y_mla = _mla(q, k, v, k_c, v_c, tq)
        y_na = _na(nq, nk, nv, nk_c, nv_c, wts["na_delta"], i)
        x = _outmlp(x, y_loc, y_mla, y_na, mods, None, wts, i, tm)

        if not last:
            yc_mla = _mla(q_c, None, None, k_c, v_c, lc)
            yc_na = _na(nq_c, None, None, nk_c, nv_c, None, i)
            xc = _outmlp(xc, yc_loc, yc_mla, yc_na, mods, ctx_row, wts, i, lc)
    return x
```

```python
import functools

import jax
import jax.numpy as jnp
import numpy as np
from jax import lax
from jax.experimental import pallas as pl
from jax.experimental.pallas import tpu as pltpu

D_MODEL = 1024
GRID_W = 64
GROUP_W = D_MODEL // 4
HEAD_DIM = 64
N_HEADS = GROUP_W // HEAD_DIM
SC_KERNEL = 3
MLA_Q_RANK = 256
MLA_KV_RANK = 128
MLA_NOPE = 64
MLA_ROPE = 32
MLA_V = 64
MLA_SCALE = (MLA_NOPE + MLA_ROPE) ** -0.5
CF_KERNEL = 31
NA_ROWS = 8
NA_COLS = 16
NA_SCALE = HEAD_DIM ** -0.5
ROPE_BASE = 10000.0
MLP_HIDDEN = 4 * D_MODEL
N_MOD = 6
EPS = 1e-6

SC_END = 3 * GROUP_W
MLA_Q_END = SC_END + MLA_Q_RANK
MLA_KV_END = MLA_Q_END + MLA_KV_RANK + MLA_ROPE
CF_END = MLA_KV_END + 2 * GROUP_W
NA_Q_END = CF_END + GROUP_W
P_IN = NA_Q_END + 2 * GROUP_W

LANES = 128
SUBLANES = 8
HEAD_PAD = LANES
MLA_PAD = N_HEADS * HEAD_PAD

NA_QROWS = 4
NA_WROWS = NA_QROWS + NA_ROWS
NA_TQ = NA_QROWS * GRID_W
NA_TK = NA_WROWS * GRID_W
NA_BLOCKS_PER_STEP = 4
HALO = 16
CONV_ROWS = 64
INPROJ_SUB = 512
MASK_VALUE = -1e30
MLA_KCHUNK = 2048
MLA_HEADS_PER_STEP = 4
LOG2E = 1.4426950408889634
MLP_CHUNK = 1024

VMEM_LIMIT = 56 * 1024 * 1024


def _cparams(n_axes):
    return pltpu.CompilerParams(dimension_semantics=("arbitrary",) * n_axes,
                                vmem_limit_bytes=VMEM_LIMIT)


def _layer_spec(arr, layer, single_buffer=False):
    nd = arr.ndim - 1
    kwargs = {"pipeline_mode": pl.Buffered(1)} if single_buffer else {}
    return pl.BlockSpec((None,) + arr.shape[1:], lambda *_: (layer,) + (0,) * nd, **kwargs)


def _mod_spec(mods, layer, row):
    if row is None:
        return pl.BlockSpec((None, None) + mods.shape[2:], lambda i, j: (layer, i, 0, 0))
    return pl.BlockSpec((None, None) + mods.shape[2:], lambda i, j: (layer, row, 0, 0))


def _rms(x):
    return x * lax.rsqrt(jnp.mean(x * x, axis=-1, keepdims=True) + EPS)


def _dot(a, b):
    return jnp.dot(a, b, preferred_element_type=jnp.float32)


def _dot_nt(a, b):
    return lax.dot_general(a, b, (((1,), (1,)), ((), ())), preferred_element_type=jnp.float32)


def _mod_kernel(cs_ref, w_ref, b_ref, o_ref):
    cs = cs_ref[...]
    a = (cs * jax.nn.sigmoid(cs)).astype(jnp.bfloat16)
    o_ref[0] = _dot(a, w_ref[0].astype(jnp.bfloat16)) + b_ref[0]


def _modulation(cs, w_mod, b_mod):
    depth, d, n = w_mod.shape
    bn = 1024
    return pl.pallas_call(
        _mod_kernel,
        grid=(depth, n // bn),
        in_specs=[pl.BlockSpec(cs.shape, lambda l, j: (0, 0)),
                  pl.BlockSpec((1, d, bn), lambda l, j: (l, 0, j)),
                  pl.BlockSpec((1, 1, bn), lambda l, j: (l, 0, j))],
        out_specs=pl.BlockSpec((1, cs.shape[0], bn), lambda l, j: (l, 0, j)),
        out_shape=jax.ShapeDtypeStruct((depth, cs.shape[0], n), jnp.float32),
        compiler_params=_cparams(2), name="modulation",
    )(cs, w_mod, b_mod.reshape(depth, 1, n))


def _inproj_kernel(x_ref, xp_ref, xn_ref, mod_ref, gpre_ref, wsc_in_ref, wmla_in_ref, wcf_in_ref, wna_in_ref,
                   gq_ref, wuq_ref, gkv_ref, wukv_ref,
                   cq_ref, sq_ref, tk_ref, wsc_ref, wdw_ref, bdw_ref, lng_ref, lnb_ref, wpw_ref,
                   yloc_ref, q_ref, k_ref, v_ref, nq_ref, nk_ref, nv_ref,
                   h_ext, ext_sc, ext_cf, shifted, conv_out, *, tm, sub):
    j = pl.program_id(1)
    sh1 = mod_ref[0:1, :]
    sc1 = mod_ref[1:2, :]
    gain = gpre_ref[...] * (1.0 + sc1)

    def modulated(x):
        return (_rms(x) * gain + sh1).astype(jnp.bfloat16)

    n_sub = tm // sub
    n_ext = sub + 2 * HALO
    for t in range(n_sub):
        r0 = t * sub
        rows = slice(r0, r0 + sub)
        x_prev = xp_ref[0] if t == 0 else x_ref[0, r0 - HALO:r0, :]
        x_next = xn_ref[0] if t == n_sub - 1 else x_ref[0, r0 + sub:r0 + sub + HALO, :]
        h_ext[t, 0:HALO, :] = modulated(x_prev)
        h_ext[t, HALO:HALO + sub, :] = modulated(x_ref[0, rows, :])
        h_ext[t, HALO + sub:n_ext, :] = modulated(x_next)
        h_all = h_ext[t]
        h_main = h_ext[t, HALO:HALO + sub, :]
        u_cf = _dot(h_all, wcf_in_ref[...])
        u_sc = _dot(h_all, wsc_in_ref[...])
        u_mla = _dot(h_main, wmla_in_ref[...])
        u_na = _dot(h_main, wna_in_ref[...])

        row = lax.broadcasted_iota(jnp.int32, (n_ext, 1), 0)
        lo_row = jnp.where(j > 0, 0, HALO) if t == 0 else 0
        hi_row = jnp.where(j < pl.num_programs(1) - 1, n_ext, sub + HALO) if t == n_sub - 1 else n_ext
        in_seq = (row >= lo_row) & (row < hi_row)
        ext_sc[t] = jnp.where(in_seq, u_sc[:, 2 * GROUP_W:3 * GROUP_W] * u_sc[:, 0:GROUP_W], 0.0)
        ext_cf[t] = jnp.where(in_seq, u_cf[:, 0:GROUP_W] * jax.nn.sigmoid(u_cf[:, GROUP_W:2 * GROUP_W]), 0.0)

        acc = jnp.zeros((sub, GROUP_W), jnp.float32)
        for k in range(SC_KERNEL):
            off = HALO - SC_KERNEL // 2 + k
            acc = acc + ext_sc[t, off:off + sub, :] * wsc_ref[k:k + 1, :]
        y_sc = u_sc[HALO:HALO + sub, GROUP_W:2 * GROUP_W] * acc

        base = HALO - CF_KERNEL // 2
        s_len = shifted.shape[2]
        for r in range(1, SUBLANES):
            shifted[t, r] = ext_cf[t, r:r + s_len, :]
        for c0 in range(0, sub, CONV_ROWS):
            acc = jnp.zeros((CONV_ROWS, GROUP_W), jnp.float32)
            for k in range(CF_KERNEL):
                r, lo = (base + k) % SUBLANES, (base + k) // SUBLANES * SUBLANES + c0
                src = ext_cf[t, lo:lo + CONV_ROWS, :] if r == 0 else shifted[t, r, lo:lo + CONV_ROWS, :]
                acc = acc + src * wdw_ref[k:k + 1, :]
            conv_out[t, c0:c0 + CONV_ROWS, :] = acc
        y = conv_out[t] + bdw_ref[...]
        mu = jnp.mean(y, axis=-1, keepdims=True)
        yc = y - mu
        var = jnp.mean(yc * yc, axis=-1, keepdims=True)
        z = (yc * lax.rsqrt(var + EPS)) * lng_ref[...] + lnb_ref[...]
        z = z * jax.nn.sigmoid(z)
        y_cf = _dot(z.astype(jnp.bfloat16), wpw_ref[...])
        yloc_ref[0, rows, 0:GROUP_W] = y_sc.astype(jnp.bfloat16)
        yloc_ref[0, rows, GROUP_W:2 * GROUP_W] = y_cf.astype(jnp.bfloat16)

        nq_ref[0, rows, :] = u_na[:, 0:GROUP_W].astype(jnp.bfloat16)
        nk_ref[0, rows, :] = u_na[:, GROUP_W:2 * GROUP_W].astype(jnp.bfloat16)
        nv_ref[0, rows, :] = u_na[:, 2 * GROUP_W:3 * GROUP_W].astype(jnp.bfloat16)

        cqn = (_rms(u_mla[:, 0:MLA_Q_RANK]) * gq_ref[...]).astype(jnp.bfloat16)
        qf = _dot(cqn, wuq_ref[...])
        cq_t = cq_ref[rows, :]
        sq_t = sq_ref[rows, :]
        for hd in range(N_HEADS):
            lo = hd * HEAD_PAD
            qh = qf[:, lo:lo + HEAD_PAD] * cq_t + qf[:, MLA_PAD + lo:MLA_PAD + lo + HEAD_PAD] * sq_t
            q_ref[0, rows, lo:lo + HEAD_PAD] = qh.astype(jnp.bfloat16)

        ckv_lo = MLA_Q_RANK
        rope_lo = MLA_Q_RANK + MLA_KV_RANK
        ckvn = (_rms(u_mla[:, ckv_lo:rope_lo]) * gkv_ref[...]).astype(jnp.bfloat16)
        kvf = _dot(ckvn, wukv_ref[...])
        g = u_mla[:, rope_lo:rope_lo + LANES] * tk_ref[rows, :]
        lane = lax.broadcasted_iota(jnp.int32, g.shape, 1)
        rot = pltpu.roll(g, 2 * MLA_ROPE, axis=1) + g
        kr = jnp.where((lane >= MLA_NOPE) & (lane < MLA_NOPE + MLA_ROPE), rot, 0.0)
        for hd in range(N_HEADS):
            lo = hd * HEAD_PAD
            k_ref[0, rows, lo:lo + HEAD_PAD] = (kvf[:, lo:lo + HEAD_PAD] + kr).astype(jnp.bfloat16)
            vh = kvf[:, MLA_PAD + lo:MLA_PAD + lo + HEAD_PAD]
            v_ref[0, rows, lo:lo + HEAD_PAD] = jnp.where(lane == MLA_V, 1.0, vh).astype(jnp.bfloat16)


def _inproj(x, mods, mod_row, wts, layer, tabs, tm):
    b, l, d = x.shape
    grid = (b, l // tm)
    hb = tm // HALO
    nhb = l // HALO
    tok = lambda w: pl.BlockSpec((1, tm, w), lambda i, j: (i, j, 0))
    prev = pl.BlockSpec((1, HALO, d), lambda i, j: (i, jnp.maximum(j * hb - 1, 0), 0))
    nxt = pl.BlockSpec((1, HALO, d), lambda i, j: (i, jnp.minimum((j + 1) * hb, nhb - 1), 0))
    tab = pl.BlockSpec((tm, LANES), lambda i, j: (j, 0))
    f32, bf16 = jnp.float32, jnp.bfloat16
    widths = [2 * GROUP_W, MLA_PAD, MLA_PAD, MLA_PAD, GROUP_W, GROUP_W, GROUP_W]
    names = ["g_pre_mix", "w_in_sc", "w_in_mla", "w_in_cf", "w_in_na", "g_q", "w_uq", "g_kv", "w_ukv"]
    conv_names = ["w_sc", "cf_w_dw", "cf_b_dw", "cf_ln_g", "cf_ln_b", "cf_w_pw"]
    sub = min(tm, INPROJ_SUB)
    n_sub = tm // sub
    s_len = (HALO - CF_KERNEL // 2 + CF_KERNEL - 1) // SUBLANES * SUBLANES + sub
    return pl.pallas_call(
        functools.partial(_inproj_kernel, tm=tm, sub=sub),
        grid=grid,
        in_specs=[tok(d), prev, nxt, _mod_spec(mods, layer, mod_row)]
                 + [_layer_spec(wts[n], layer) for n in names] + [tab, tab, tab]
                 + [_layer_spec(wts[n], layer) for n in conv_names],
        out_specs=[tok(w) for w in widths],
        out_shape=[jax.ShapeDtypeStruct((b, l, w), bf16) for w in widths],
        scratch_shapes=[pltpu.VMEM((n_sub, sub + 2 * HALO, d), bf16),
                        pltpu.VMEM((n_sub, sub + 2 * HALO, GROUP_W), f32),
                        pltpu.VMEM((n_sub, sub + 2 * HALO, GROUP_W), f32),
                        pltpu.VMEM((n_sub, SUBLANES, s_len, GROUP_W), f32),
                        pltpu.VMEM((n_sub, sub, GROUP_W), f32)],
        compiler_params=_cparams(2), name="inproj",
    )(x, x, x, mods, *[wts[n] for n in names], tabs["cq"], tabs["sq"], tabs["tk"],
      *[wts[n] for n in conv_names])


def _mla_kernel(*refs, has_lat):
    if has_lat:
        q_ref, kl_ref, vl_ref, kc_ref, vc_ref, o_ref = refs
    else:
        q_ref, kc_ref, vc_ref, o_ref = refs
    chunks = [(kc_ref, vc_ref, 0, kc_ref.shape[1])]
    if has_lat:
        n_lat = kl_ref.shape[1]
        chunks += [(kl_ref, vl_ref, lo, MLA_KCHUNK) for lo in range(0, n_lat, MLA_KCHUNK)]
    c2 = MLA_SCALE * LOG2E
    outs = []
    for hd in range(MLA_HEADS_PER_STEP):
        lanes = slice(hd * HEAD_PAD, (hd + 1) * HEAD_PAD)
        q = q_ref[0, :, lanes]
        m = o = None
        for k_ref, v_ref, lo, n in chunks:
            s = _dot_nt(q, k_ref[0, lo:lo + n, lanes])
            m_chunk = jnp.max(s, axis=-1, keepdims=True)
            m_new = m_chunk if m is None else jnp.maximum(m, m_chunk)
            p = jnp.exp2((s - m_new) * c2).astype(jnp.bfloat16)
            pv = _dot(p, v_ref[0, lo:lo + n, lanes])
            o = pv if o is None else o * jnp.exp2((m - m_new) * c2) + pv
            m = m_new
        outs.append(o * (1.0 / o[:, MLA_V:MLA_V + 1]))
    lane = lax.broadcasted_iota(jnp.int32, outs[0].shape, 1)
    for pair in range(MLA_HEADS_PER_STEP // 2):
        packed = jnp.where(lane < MLA_V, outs[2 * pair], pltpu.roll(outs[2 * pair + 1], MLA_V, axis=1))
        o_ref[0, :, pair * HEAD_PAD:(pair + 1) * HEAD_PAD] = packed.astype(jnp.bfloat16)


def _mla(q, k_lat, v_lat, k_ctx, v_ctx, tq):
    b, lq, _ = q.shape
    lc = k_ctx.shape[1]
    has_lat = k_lat is not None
    hw = MLA_HEADS_PER_STEP * HEAD_PAD
    grid = (b, N_HEADS // MLA_HEADS_PER_STEP, lq // tq)
    qspec = pl.BlockSpec((1, tq, hw), lambda i, h, j: (i, j, h))
    kvspec = lambda n: pl.BlockSpec((1, n, hw), lambda i, h, j: (i, 0, h))
    in_specs = [qspec]
    args = [q]
    if has_lat:
        in_specs += [kvspec(k_lat.shape[1])] * 2
        args += [k_lat, v_lat]
    in_specs += [kvspec(lc)] * 2
    args += [k_ctx, v_ctx]
    return pl.pallas_call(
        functools.partial(_mla_kernel, has_lat=has_lat),
        grid=grid,
        in_specs=in_specs,
        out_specs=pl.BlockSpec((1, tq, MLA_HEADS_PER_STEP * MLA_V), lambda i, h, j: (i, j, h)),
        out_shape=jax.ShapeDtypeStruct((b, lq, N_HEADS * MLA_V), jnp.bfloat16),
        compiler_params=_cparams(3), name="mla_attn",
    )(*args)


def _na_kernel(*refs, has_local):
    if has_local:
        q_ref, k_ref, v_ref, kc_ref, vc_ref, delta_ref, o_ref, bias_ref, planes_ref = refs

        @pl.when((pl.program_id(0) == 0) & (pl.program_id(1) == 0))
        def _():
            w = GRID_W
            lane = lax.broadcasted_iota(jnp.int32, (w, 2 * w), 1)
            left = lane < w
            qc = lax.broadcasted_iota(jnp.int32, (w, 2 * w), 0)
            kc = lane & (w - 1)
            cstart = jnp.clip(qc - NA_COLS // 2, 0, w - NA_COLS)
            in_cols = (kc >= cstart) & (kc < cstart + NA_COLS)
            for hd, o in np.ndindex(N_HEADS, NA_MASKED_PLANE):
                vec = jnp.broadcast_to(delta_ref[hd, o:o + 1, :], (w, 2 * w))
                lo_half = pltpu.roll(vec, w + 1, axis=1, stride=1, stride_axis=0)
                hi_half = pltpu.roll(vec, 1, axis=1, stride=1, stride_axis=0)
                planes_ref[hd, o] = jnp.where(in_cols, jnp.where(left, lo_half, hi_half), MASK_VALUE)
            for hd in range(N_HEADS):
                planes_ref[hd, NA_MASKED_PLANE] = jnp.full((w, 2 * w), MASK_VALUE, jnp.float32)
            for c, qi, kp in np.ndindex(3, NA_QROWS, NA_WROWS // 2):
                pa, pb = (int(v) for v in _NA_PLANE[c, qi, 2 * kp:2 * kp + 2])
                for hd in range(N_HEADS):
                    blk2 = planes_ref[hd, pa] if pa == pb else jnp.where(left, planes_ref[hd, pa],
                                                                         planes_ref[hd, pb])
                    bias_ref[c, hd, qi * GRID_W:(qi + 1) * GRID_W,
                             2 * kp * GRID_W:2 * (kp + 1) * GRID_W] = blk2

    else:
        q_ref, kc_ref, vc_ref, o_ref = refs
    kc = kc_ref[0]
    vc = vc_ref[0]
    lane = lax.broadcasted_iota(jnp.int32, (NA_TQ, q_ref.shape[2]), 1)
    c2 = NA_SCALE * LOG2E
    n_blocks = q_ref.shape[1] // NA_TQ
    for sb in range(n_blocks):
        rows = slice(sb * NA_TQ, (sb + 1) * NA_TQ)
        if has_local:
            blk = pl.program_id(1) * n_blocks + sb
            nblk = pl.num_programs(1) * n_blocks
            wrow = jnp.clip(blk * NA_QROWS - NA_ROWS // 2, 0, GRID_W - NA_WROWS)
            start = pl.multiple_of(wrow * GRID_W, GRID_W)
            case = jnp.where(blk == 0, 0, jnp.where(blk == nblk - 1, 2, 1))
            kwin = k_ref[0, pl.ds(start, NA_TK), :]
            vwin = v_ref[0, pl.ds(start, NA_TK), :]
        q = q_ref[0, rows, :]
        out = jnp.zeros(q.shape, jnp.float32)
        for hd in range(N_HEADS):
            in_head = (lane >= hd * HEAD_DIM) & (lane < (hd + 1) * HEAD_DIM)
            qm = jnp.where(in_head, q, jnp.zeros_like(q))
            s_c = _dot_nt(qm, kc) * c2
            m = jnp.max(s_c, axis=-1, keepdims=True)
            if has_local:
                s_l = _dot_nt(qm, kwin) * c2 + bias_ref[case, hd]
                m = jnp.maximum(m, jnp.max(s_l, axis=-1, keepdims=True))
            p_c = jnp.exp2(s_c - m)
            den = jnp.sum(p_c, axis=-1, keepdims=True)
            o = _dot(p_c.astype(jnp.bfloat16), vc)
            if has_local:
                p_l = jnp.exp2(s_l - m)
                den = den + jnp.sum(p_l, axis=-1, keepdims=True)
                o = o + _dot(p_l.astype(jnp.bfloat16), vwin)
            out = jnp.where(in_head, o * (1.0 / den), out)
        o_ref[0, rows, :] = out.astype(jnp.bfloat16)


def _na(q, k, v, k_ctx, v_ctx, planes, layer):
    b, l, w = q.shape
    lc = k_ctx.shape[1]
    has_local = k is not None
    tq = NA_TQ * NA_BLOCKS_PER_STEP if has_local else NA_TQ
    grid = (b, l // tq)
    qspec = pl.BlockSpec((1, tq, w), lambda i, j: (i, j, 0))
    full = lambda n: pl.BlockSpec((1, n, w), lambda i, j: (i, 0, 0))
    in_specs = [qspec]
    args = [q]
    if has_local:
        in_specs += [full(l), full(l)]
        args += [k, v]
    in_specs += [full(lc), full(lc)]
    args += [k_ctx, v_ctx]
    scratch = []
    if has_local:
        in_specs += [_layer_spec(planes, layer)]
        args += [planes]
        scratch = [pltpu.VMEM((3, N_HEADS, NA_TQ, NA_TK), jnp.float32),
                   pltpu.VMEM((N_HEADS, NA_MASKED_PLANE + 1, GRID_W, 2 * GRID_W), jnp.float32)]
    return pl.pallas_call(
        functools.partial(_na_kernel, has_local=has_local),
        grid=grid,
        in_specs=in_specs,
        out_specs=qspec,
        out_shape=jax.ShapeDtypeStruct((b, l, w), jnp.bfloat16),
        scratch_shapes=scratch,
        compiler_params=_cparams(2), name="na_attn",
    )(*args)


NA_MASKED_PLANE = 2 * NA_ROWS - 1


def _na_plane_index():
    rows = GRID_W
    idx = np.full((3, NA_QROWS, NA_WROWS), NA_MASKED_PLANE, np.int32)
    for c, r0 in enumerate((0, NA_QROWS, rows - NA_QROWS)):
        wrow = min(max(r0 - NA_ROWS // 2, 0), rows - NA_WROWS)
        for qi in range(NA_QROWS):
            r = r0 + qi
            band = min(max(r - NA_ROWS // 2, 0), rows - NA_ROWS)
            for kj in range(NA_WROWS):
                kr = wrow + kj
                if band <= kr < band + NA_ROWS:
                    idx[c, qi, kj] = kr - r + (NA_ROWS - 1)
    return idx


_NA_PLANE = _na_plane_index()


def _na_delta_table(rpb):
    w = GRID_W
    delta = np.arange(2 * w) - (w - 1)
    return rpb[..., np.clip(delta + (NA_COLS - 1), 0, 2 * NA_COLS - 2)] * LOG2E


def _outmlp_kernel(x_ref, yloc_ref, ymla_ref, yna_ref, mod_ref, gpm_ref, gpre_ref, gpost_ref,
                   wout_ref, w1_ref, w2_ref, o_ref):
    ga1 = mod_ref[2:3, :]
    sh2 = mod_ref[3:4, :]
    sc2 = mod_ref[4:5, :]
    ga2 = mod_ref[5:6, :]
    g_mix = gpm_ref[...] * ga1
    g_mlp_in = gpre_ref[...] * (1.0 + sc2)
    g_mlp_out = gpost_ref[...] * ga2
    x = x_ref[0]
    ycat = jnp.concatenate([yloc_ref[0, :, 0:GROUP_W], ymla_ref[0], yloc_ref[0, :, GROUP_W:2 * GROUP_W],
                            yna_ref[0]], axis=-1)
    y = _dot(ycat, wout_ref[...])
    x1 = x + _rms(y) * g_mix
    hm = (_rms(x1) * g_mlp_in + sh2).astype(jnp.bfloat16)
    acc = jnp.zeros(x.shape, jnp.float32)
    for c in range(MLP_HIDDEN // MLP_CHUNK):
        lo = c * MLP_CHUNK
        hj = jnp.maximum(_dot(hm, w1_ref[:, lo:lo + MLP_CHUNK]), 0.0)
        acc = acc + _dot((hj * hj).astype(jnp.bfloat16), w2_ref[lo:lo + MLP_CHUNK, :])
    o_ref[0] = x1 + _rms(acc) * g_mlp_out


def _outmlp(x, yloc, ymla, yna, mods, mod_row, wts, layer, tm):
    b, l, d = x.shape
    grid = (b, l // tm)
    tok = lambda w: pl.BlockSpec((1, tm, w), lambda i, j: (i, j, 0))
    gains = ["g_post_mix", "g_pre_mlp", "g_post_mlp"]
    mats = ["w_out", "w_mlp1", "w_mlp2"]
    return pl.pallas_call(
        _outmlp_kernel,
        grid=grid,
        in_specs=[tok(d), tok(2 * GROUP_W), tok(GROUP_W), tok(GROUP_W), _mod_spec(mods, layer, mod_row)]
                 + [_layer_spec(wts[n], layer) for n in gains]
                 + [_layer_spec(wts[n], layer, single_buffer=True) for n in mats],
        out_specs=tok(d),
        out_shape=jax.ShapeDtypeStruct((b, l, d), jnp.float32),
        compiler_params=_cparams(2), name="outproj_mlp",
    )(x, yloc, ymla, yna, mods, *[wts[n] for n in gains], *[wts[n] for n in mats])


def _swap_halves_signed(w):
    q = MLA_ROPE // 4
    x1, x2, x3, x4 = (w[..., i * q:(i + 1) * q] for i in range(4))
    return jnp.concatenate([-x2, x1, -x4, x3], axis=-1)


def _prep_weights(p):
    bf16 = jnp.bfloat16
    w_in = p["w_in"]
    depth, d, _ = w_in.shape
    k_rope = w_in[..., MLA_Q_END + MLA_KV_RANK:MLA_KV_END]
    z32 = jnp.zeros((depth, d, MLA_ROPE), w_in.dtype)
    w_in_sc = w_in[..., :SC_END].astype(bf16)
    w_in_mla = jnp.concatenate([w_in[..., SC_END:MLA_Q_END + MLA_KV_RANK],
                                k_rope, z32, _swap_halves_signed(k_rope), z32], axis=-1).astype(bf16)
    w_in_cf = w_in[..., MLA_KV_END:CF_END].astype(bf16)
    w_in_na = w_in[..., CF_END:].astype(bf16)

    wq = p["w_uq"].reshape(depth, MLA_Q_RANK, N_HEADS, MLA_NOPE + MLA_ROPE)
    zq = jnp.zeros((depth, MLA_Q_RANK, N_HEADS, HEAD_PAD - MLA_NOPE - MLA_ROPE), wq.dtype)
    zn = jnp.zeros((depth, MLA_Q_RANK, N_HEADS, MLA_NOPE), wq.dtype)
    q_main = jnp.concatenate([wq, zq], axis=-1).reshape(depth, MLA_Q_RANK, MLA_PAD)
    q_swap = jnp.concatenate([zn, _swap_halves_signed(wq[..., MLA_NOPE:]), zq], axis=-1)
    w_uq_p = jnp.concatenate([q_main, q_swap.reshape(depth, MLA_Q_RANK, MLA_PAD)], axis=-1).astype(bf16)

    wkv = p["w_ukv"].reshape(depth, MLA_KV_RANK, N_HEADS, MLA_NOPE + MLA_V)
    zk = jnp.zeros((depth, MLA_KV_RANK, N_HEADS, HEAD_PAD - MLA_NOPE), wkv.dtype)
    k_main = jnp.concatenate([wkv[..., :MLA_NOPE], zk], axis=-1).reshape(depth, MLA_KV_RANK, MLA_PAD)
    v_main = jnp.concatenate([wkv[..., MLA_NOPE:], zk], axis=-1).reshape(depth, MLA_KV_RANK, MLA_PAD)
    w_ukv_p = jnp.concatenate([k_main, v_main], axis=-1).astype(bf16)

    row = lambda a: a.reshape(depth, 1, -1)
    return {
        "g_pre_mix": row(p["g_pre_mix"]), "w_in_sc": w_in_sc, "w_in_mla": w_in_mla, "w_in_cf": w_in_cf,
        "w_in_na": w_in_na, "g_q": row(p["g_q"]), "w_uq": w_uq_p,
        "g_kv": row(p["g_kv"]), "w_ukv": w_ukv_p, "w_sc": p["w_sc"], "cf_w_dw": p["cf_w_dw"],
        "cf_b_dw": row(p["cf_b_dw"]), "cf_ln_g": row(p["cf_ln_g"]), "cf_ln_b": row(p["cf_ln_b"]),
        "cf_w_pw": p["cf_w_pw"].astype(bf16), "w_out": p["w_out"].astype(bf16), "g_post_mix": row(p["g_post_mix"]),
        "g_pre_mlp": row(p["g_pre_mlp"]), "w_mlp1": p["w_mlp1"].astype(bf16),
        "w_mlp2": p["w_mlp2"].astype(bf16), "g_post_mlp": row(p["g_post_mlp"]),
        "na_delta": _na_delta_table(p["na_rpb"]),
    }


def _rope_tables(l, rotate):
    f32 = np.float32
    ones = np.ones((l, MLA_NOPE), f32)
    zpad = np.zeros((l, HEAD_PAD - MLA_NOPE - MLA_ROPE), f32)
    z32 = np.zeros((l, MLA_ROPE), f32)
    if rotate:
        pos = np.arange(l)
        n_freq = MLA_ROPE // 4
        inv_freq = (f32(ROPE_BASE) ** (-np.arange(n_freq, dtype=f32) / f32(n_freq))).astype(f32)
        ang_row = (pos // GRID_W).astype(f32)[:, None] * inv_freq
        ang_col = (pos % GRID_W).astype(f32)[:, None] * inv_freq
        cos = np.concatenate([np.cos(ang_row)] * 2 + [np.cos(ang_col)] * 2, axis=-1).astype(f32)
        sin = np.concatenate([np.sin(ang_row)] * 2 + [np.sin(ang_col)] * 2, axis=-1).astype(f32)
    else:
        cos = np.ones((l, MLA_ROPE), f32)
        sin = z32
    return {"cq": jnp.asarray(np.concatenate([ones, cos, zpad], axis=-1)),
            "sq": jnp.asarray(np.concatenate([np.zeros_like(ones), sin, zpad], axis=-1)),
            "tk": jnp.asarray(np.concatenate([cos, z32, sin, z32], axis=-1))}


def kernel(x, c, ctx, c_ctx, w_mod, b_mod, g_pre_mix, w_in, w_sc, g_q, w_uq, g_kv, w_ukv, cf_w_dw,
           cf_b_dw, cf_ln_g, cf_ln_b, cf_w_pw, na_rpb, w_out, g_post_mix, g_pre_mlp, w_mlp1, w_mlp2,
           g_post_mlp):
    b, l, d = x.shape
    lc = ctx.shape[1]
    depth = w_mod.shape[0]
    wts = _prep_weights({
        "g_pre_mix": g_pre_mix, "w_in": w_in, "w_sc": w_sc, "g_q": g_q, "w_uq": w_uq, "g_kv": g_kv,
        "w_ukv": w_ukv, "cf_w_dw": cf_w_dw, "cf_b_dw": cf_b_dw, "cf_ln_g": cf_ln_g, "cf_ln_b": cf_ln_b,
        "cf_w_pw": cf_w_pw, "na_rpb": na_rpb, "w_out": w_out, "g_post_mix": g_post_mix,
        "g_pre_mlp": g_pre_mlp, "w_mlp1": w_mlp1, "w_mlp2": w_mlp2, "g_post_mlp": g_post_mlp})

    n_rows = 8
    cs = jnp.concatenate([c, c_ctx[None, :], jnp.zeros((n_rows - b - 1, d), c.dtype)], axis=0)
    mods = _modulation(cs, w_mod, b_mod).reshape(depth, n_rows, N_MOD, d)
    ctx_row = b
    tabs_lat = _rope_tables(l, True)
    tabs_ctx = _rope_tables(lc, False)

    tm = 1024
    tq = 1024
    xc = ctx
    for i in range(depth):
        last = i == depth - 1
        y_loc, q, k, v, nq, nk, nv = _inproj(x, mods, None, wts, i, tabs_lat, tm)
        yc_loc, q_c, k_c, v_c, nq_c, nk_c, nv_c = _inproj(xc, mods, ctx_row, wts, i, tabs_ctx, lc)

        y_mla = _mla(q, k, v, k_c, v_c, tq)
        y_na = _na(nq, nk, nv, nk_c, nv_c, wts["na_delta"], i)
        x = _outmlp(x, y_loc, y_mla, y_na, mods, None, wts, i, tm)

        if not last:
            yc_mla = _mla(q_c, None, None, k_c, v_c, lc)
            yc_na = _na(nq_c, None, None, nk_c, nv_c, None, i)
            xc = _outmlp(xc, yc_loc, yc_mla, yc_na, mods, ctx_row, wts, i, lc)
    return x
```

```python
import functools

import jax
import jax.numpy as jnp
import numpy as np
from jax import lax
from jax.experimental import pallas as pl
from jax.experimental.pallas import tpu as pltpu

D_MODEL = 1024
GRID_W = 64
GROUP_W = D_MODEL // 4
HEAD_DIM = 64
N_HEADS = GROUP_W // HEAD_DIM
SC_KERNEL = 3
MLA_Q_RANK = 256
MLA_KV_RANK = 128
MLA_NOPE = 64
MLA_ROPE = 32
MLA_V = 64
MLA_SCALE = (MLA_NOPE + MLA_ROPE) ** -0.5
CF_KERNEL = 31
NA_ROWS = 8
NA_COLS = 16
NA_SCALE = HEAD_DIM ** -0.5
ROPE_BASE = 10000.0
MLP_HIDDEN = 4 * D_MODEL
N_MOD = 6
EPS = 1e-6

SC_END = 3 * GROUP_W
MLA_Q_END = SC_END + MLA_Q_RANK
MLA_KV_END = MLA_Q_END + MLA_KV_RANK + MLA_ROPE
CF_END = MLA_KV_END + 2 * GROUP_W
NA_Q_END = CF_END + GROUP_W
P_IN = NA_Q_END + 2 * GROUP_W

LANES = 128
SUBLANES = 8
HEAD_PAD = LANES
MLA_PAD = N_HEADS * HEAD_PAD

NA_QROWS = 4
NA_WROWS = NA_QROWS + NA_ROWS
NA_TQ = NA_QROWS * GRID_W
NA_TK = NA_WROWS * GRID_W
NA_BLOCKS_PER_STEP = 4
HALO = 16
CONV_ROWS = 64
INPROJ_SUB = 512
MASK_VALUE = -1e30
MLA_KCHUNK = 2048
MLA_HEADS_PER_STEP = 4
LOG2E = 1.4426950408889634
MLP_CHUNK = 1024
MOD_BLOCK = 1024

VMEM_LIMIT = 56 * 1024 * 1024


def _cparams(n_axes):
    return pltpu.CompilerParams(dimension_semantics=("arbitrary",) * n_axes,
                                vmem_limit_bytes=VMEM_LIMIT)


def _layer_spec(arr, layer, single_buffer=False):
    nd = arr.ndim - 1
    if nd == 1:
        return pl.BlockSpec(arr.shape, lambda *_: (0, 0))
    kwargs = {"pipeline_mode": pl.Buffered(1)} if single_buffer else {}
    return pl.BlockSpec((None,) + arr.shape[1:], lambda *_: (layer,) + (0,) * nd, **kwargs)


def _mod_spec(mods, layer, row):
    if row is None:
        return pl.BlockSpec((None, None) + mods.shape[2:], lambda i, j: (layer, i, 0, 0))
    return pl.BlockSpec((None, None) + mods.shape[2:], lambda i, j: (layer, row, 0, 0))


def _rms(x):
    return x * lax.rsqrt(jnp.mean(x * x, axis=-1, keepdims=True) + EPS)


def _dot(a, b):
    return jnp.dot(a, b, preferred_element_type=jnp.float32)


def _dot_nt(a, b):
    return lax.dot_general(a, b, (((1,), (1,)), ((), ())), preferred_element_type=jnp.float32)


def _mod_kernel(cs_ref, w_ref, b_ref, o_ref):
    cs = cs_ref[...]
    a = (cs * jax.nn.sigmoid(cs)).astype(jnp.bfloat16)
    o_ref[0] = _dot(a, w_ref[0].astype(jnp.bfloat16)) + b_ref[0]


def _modulation(cs, w_mod, b_mod):
    depth, d, n = w_mod.shape
    bn = MOD_BLOCK
    return pl.pallas_call(
        _mod_kernel,
        grid=(depth, n // bn),
        in_specs=[pl.BlockSpec(cs.shape, lambda l, j: (0, 0)),
                  pl.BlockSpec((1, d, bn), lambda l, j: (l, 0, j)),
                  pl.BlockSpec((1, 1, bn), lambda l, j: (l, 0, j))],
        out_specs=pl.BlockSpec((1, cs.shape[0], bn), lambda l, j: (l, 0, j)),
        out_shape=jax.ShapeDtypeStruct((depth, cs.shape[0], n), jnp.float32),
        compiler_params=_cparams(2), name="modulation",
    )(cs, w_mod, b_mod.reshape(depth, 1, n))


def _inproj_kernel(x_ref, xp_ref, xn_ref, mod_ref, gpre_ref, wsc_in_ref, wmla_in_ref, wcf_in_ref, wna_in_ref,
                   gq_ref, wuq_ref, gkv_ref, wukv_ref,
                   cq_ref, sq_ref, tk_ref, wsc_ref, wdw_ref, bdw_ref, lng_ref, lnb_ref, wpw_ref,
                   yloc_ref, q_ref, k_ref, v_ref, nq_ref, nk_ref, nv_ref,
                   h_ext, ext_sc, ext_cf, shifted, conv_out, *, tm, sub, layer):
    j = pl.program_id(1)
    this = slice(layer, layer + 1)
    sh1 = mod_ref[0:1, :]
    sc1 = mod_ref[1:2, :]
    gain = gpre_ref[this, :] * (1.0 + sc1)

    def modulated(x):
        return (_rms(x) * gain + sh1).astype(jnp.bfloat16)

    n_sub = tm // sub
    n_ext = sub + 2 * HALO
    for t in range(n_sub):
        r0 = t * sub
        rows = slice(r0, r0 + sub)
        x_prev = xp_ref[0] if t == 0 else x_ref[0, r0 - HALO:r0, :]
        x_next = xn_ref[0] if t == n_sub - 1 else x_ref[0, r0 + sub:r0 + sub + HALO, :]
        h_ext[t, 0:HALO, :] = modulated(x_prev)
        h_ext[t, HALO:HALO + sub, :] = modulated(x_ref[0, rows, :])
        h_ext[t, HALO + sub:n_ext, :] = modulated(x_next)
        h_all = h_ext[t]
        h_main = h_ext[t, HALO:HALO + sub, :]
        u_cf = _dot(h_all, wcf_in_ref[...])
        u_sc = _dot(h_all, wsc_in_ref[...])
        u_mla = _dot(h_main, wmla_in_ref[...])
        u_na = _dot(h_main, wna_in_ref[...])

        row = lax.broadcasted_iota(jnp.int32, (n_ext, 1), 0)
        lo_row = jnp.where(j > 0, 0, HALO) if t == 0 else 0
        hi_row = jnp.where(j < pl.num_programs(1) - 1, n_ext, sub + HALO) if t == n_sub - 1 else n_ext
        in_seq = (row >= lo_row) & (row < hi_row)
        ext_sc[t] = jnp.where(in_seq, u_sc[:, 2 * GROUP_W:3 * GROUP_W] * u_sc[:, 0:GROUP_W], 0.0)
        ext_cf[t] = jnp.where(in_seq, u_cf[:, 0:GROUP_W] * jax.nn.sigmoid(u_cf[:, GROUP_W:2 * GROUP_W]), 0.0)

        acc = jnp.zeros((sub, GROUP_W), jnp.float32)
        for k in range(SC_KERNEL):
            off = HALO - SC_KERNEL // 2 + k
            acc = acc + ext_sc[t, off:off + sub, :] * wsc_ref[k:k + 1, :]
        y_sc = u_sc[HALO:HALO + sub, GROUP_W:2 * GROUP_W] * acc

        base = HALO - CF_KERNEL // 2
        s_len = shifted.shape[2]
        for r in range(1, SUBLANES):
            shifted[t, r] = ext_cf[t, r:r + s_len, :]
        for c0 in range(0, sub, CONV_ROWS):
            acc = jnp.zeros((CONV_ROWS, GROUP_W), jnp.float32)
            for k in range(CF_KERNEL):
                r, lo = (base + k) % SUBLANES, (base + k) // SUBLANES * SUBLANES + c0
                src = ext_cf[t, lo:lo + CONV_ROWS, :] if r == 0 else shifted[t, r, lo:lo + CONV_ROWS, :]
                acc = acc + src * wdw_ref[k:k + 1, :]
            conv_out[t, c0:c0 + CONV_ROWS, :] = acc
        y = conv_out[t] + bdw_ref[this, :]
        mu = jnp.mean(y, axis=-1, keepdims=True)
        yc = y - mu
        var = jnp.mean(yc * yc, axis=-1, keepdims=True)
        z = (yc * lax.rsqrt(var + EPS)) * lng_ref[this, :] + lnb_ref[this, :]
        z = z * jax.nn.sigmoid(z)
        y_cf = _dot(z.astype(jnp.bfloat16), wpw_ref[...])
        yloc_ref[0, rows, 0:GROUP_W] = y_sc.astype(jnp.bfloat16)
        yloc_ref[0, rows, GROUP_W:2 * GROUP_W] = y_cf.astype(jnp.bfloat16)

        nq_ref[0, rows, :] = u_na[:, 0:GROUP_W].astype(jnp.bfloat16)
        nk_ref[0, rows, :] = u_na[:, GROUP_W:2 * GROUP_W].astype(jnp.bfloat16)
        nv_ref[0, rows, :] = u_na[:, 2 * GROUP_W:3 * GROUP_W].astype(jnp.bfloat16)

        cqn = (_rms(u_mla[:, 0:MLA_Q_RANK]) * gq_ref[this, :]).astype(jnp.bfloat16)
        qf = _dot(cqn, wuq_ref[...])
        cq_t = cq_ref[rows, :]
        sq_t = sq_ref[rows, :]
        for hd in range(N_HEADS):
            lo = hd * HEAD_PAD
            qh = qf[:, lo:lo + HEAD_PAD] * cq_t + qf[:, MLA_PAD + lo:MLA_PAD + lo + HEAD_PAD] * sq_t
            q_ref[0, rows, lo:lo + HEAD_PAD] = qh.astype(jnp.bfloat16)

        ckv_lo = MLA_Q_RANK
        rope_lo = MLA_Q_RANK + MLA_KV_RANK
        ckvn = (_rms(u_mla[:, ckv_lo:rope_lo]) * gkv_ref[this, :]).astype(jnp.bfloat16)
        kvf = _dot(ckvn, wukv_ref[...])
        g = u_mla[:, rope_lo:rope_lo + LANES] * tk_ref[rows, :]
        lane = lax.broadcasted_iota(jnp.int32, g.shape, 1)
        rot = pltpu.roll(g, 2 * MLA_ROPE, axis=1) + g
        kr = jnp.where((lane >= MLA_NOPE) & (lane < MLA_NOPE + MLA_ROPE), rot, 0.0)
        for hd in range(N_HEADS):
            lo = hd * HEAD_PAD
            k_ref[0, rows, lo:lo + HEAD_PAD] = (kvf[:, lo:lo + HEAD_PAD] + kr).astype(jnp.bfloat16)
            vh = kvf[:, MLA_PAD + lo:MLA_PAD + lo + HEAD_PAD]
            v_ref[0, rows, lo:lo + HEAD_PAD] = jnp.where(lane == MLA_V, 1.0, vh).astype(jnp.bfloat16)


def _inproj(x, mods, mod_row, wts, layer, tabs, tm):
    b, l, d = x.shape
    grid = (b, l // tm)
    hb = tm // HALO
    nhb = l // HALO
    tok = lambda w: pl.BlockSpec((1, tm, w), lambda i, j: (i, j, 0))
    prev = pl.BlockSpec((1, HALO, d), lambda i, j: (i, jnp.maximum(j * hb - 1, 0), 0))
    nxt = pl.BlockSpec((1, HALO, d), lambda i, j: (i, jnp.minimum((j + 1) * hb, nhb - 1), 0))
    tab = pl.BlockSpec((tm, LANES), lambda i, j: (j, 0))
    f32, bf16 = jnp.float32, jnp.bfloat16
    widths = [2 * GROUP_W, MLA_PAD, MLA_PAD, MLA_PAD, GROUP_W, GROUP_W, GROUP_W]
    names = ["g_pre_mix", "w_in_sc", "w_in_mla", "w_in_cf", "w_in_na", "g_q", "w_uq", "g_kv", "w_ukv"]
    conv_names = ["w_sc", "cf_w_dw", "cf_b_dw", "cf_ln_g", "cf_ln_b", "cf_w_pw"]
    sub = min(tm, INPROJ_SUB)
    n_sub = tm // sub
    assert l % tm == 0 and tm % sub == 0 and sub % CONV_ROWS == 0 and tm % HALO == 0, (l, tm, sub)
    s_len = (HALO - CF_KERNEL // 2 + CF_KERNEL - 1) // SUBLANES * SUBLANES + sub
    return pl.pallas_call(
        functools.partial(_inproj_kernel, tm=tm, sub=sub, layer=layer),
        grid=grid,
        in_specs=[tok(d), prev, nxt, _mod_spec(mods, layer, mod_row)]
                 + [_layer_spec(wts[n], layer) for n in names] + [tab, tab, tab]
                 + [_layer_spec(wts[n], layer) for n in conv_names],
        out_specs=[tok(w) for w in widths],
        out_shape=[jax.ShapeDtypeStruct((b, l, w), bf16) for w in widths],
        scratch_shapes=[pltpu.VMEM((n_sub, sub + 2 * HALO, d), bf16),
                        pltpu.VMEM((n_sub, sub + 2 * HALO, GROUP_W), f32),
                        pltpu.VMEM((n_sub, sub + 2 * HALO, GROUP_W), f32),
                        pltpu.VMEM((n_sub, SUBLANES, s_len, GROUP_W), f32),
                        pltpu.VMEM((n_sub, sub, GROUP_W), f32)],
        compiler_params=_cparams(2), name="inproj",
    )(x, x, x, mods, *[wts[n] for n in names], tabs["cq"], tabs["sq"], tabs["tk"],
      *[wts[n] for n in conv_names])


def _mla_kernel(*refs, has_lat):
    if has_lat:
        q_ref, kl_ref, vl_ref, kc_ref, vc_ref, o_ref = refs
    else:
        q_ref, kc_ref, vc_ref, o_ref = refs
    chunks = [(kc_ref, vc_ref, 0, kc_ref.shape[1])]
    if has_lat:
        n_lat = kl_ref.shape[1]
        chunks += [(kl_ref, vl_ref, lo, MLA_KCHUNK) for lo in range(0, n_lat, MLA_KCHUNK)]
    c2 = MLA_SCALE * LOG2E
    outs = []
    for hd in range(MLA_HEADS_PER_STEP):
        lanes = slice(hd * HEAD_PAD, (hd + 1) * HEAD_PAD)
        q = q_ref[0, :, lanes]
        m = o = None
        for k_ref, v_ref, lo, n in chunks:
            s = _dot_nt(q, k_ref[0, lo:lo + n, lanes])
            m_chunk = jnp.max(s, axis=-1, keepdims=True)
            m_new = m_chunk if m is None else jnp.maximum(m, m_chunk)
            p = jnp.exp2((s - m_new) * c2).astype(jnp.bfloat16)
            pv = _dot(p, v_ref[0, lo:lo + n, lanes])
            o = pv if o is None else o * jnp.exp2((m - m_new) * c2) + pv
            m = m_new
        outs.append(o * (1.0 / o[:, MLA_V:MLA_V + 1]))
    lane = lax.broadcasted_iota(jnp.int32, outs[0].shape, 1)
    for pair in range(MLA_HEADS_PER_STEP // 2):
        packed = jnp.where(lane < MLA_V, outs[2 * pair], pltpu.roll(outs[2 * pair + 1], MLA_V, axis=1))
        o_ref[0, :, pair * HEAD_PAD:(pair + 1) * HEAD_PAD] = packed.astype(jnp.bfloat16)


def _mla(q, k_lat, v_lat, k_ctx, v_ctx, tq):
    b, lq, _ = q.shape
    lc = k_ctx.shape[1]
    has_lat = k_lat is not None
    hw = MLA_HEADS_PER_STEP * HEAD_PAD
    assert lq % tq == 0 and (not has_lat or k_lat.shape[1] % MLA_KCHUNK == 0), (lq, tq)
    grid = (b, N_HEADS // MLA_HEADS_PER_STEP, lq // tq)
    qspec = pl.BlockSpec((1, tq, hw), lambda i, h, j: (i, j, h))
    kvspec = lambda n: pl.BlockSpec((1, n, hw), lambda i, h, j: (i, 0, h))
    in_specs = [qspec]
    args = [q]
    if has_lat:
        in_specs += [kvspec(k_lat.shape[1])] * 2
        args += [k_lat, v_lat]
    in_specs += [kvspec(lc)] * 2
    args += [k_ctx, v_ctx]
    return pl.pallas_call(
        functools.partial(_mla_kernel, has_lat=has_lat),
        grid=grid,
        in_specs=in_specs,
        out_specs=pl.BlockSpec((1, tq, MLA_HEADS_PER_STEP * MLA_V), lambda i, h, j: (i, j, h)),
        out_shape=jax.ShapeDtypeStruct((b, lq, N_HEADS * MLA_V), jnp.bfloat16),
        compiler_params=_cparams(3), name="mla_attn",
    )(*args)


def _na_kernel(*refs, has_local):
    if has_local:
        q_ref, k_ref, v_ref, kc_ref, vc_ref, delta_ref, o_ref, bias_ref, planes_ref = refs

        @pl.when((pl.program_id(0) == 0) & (pl.program_id(1) == 0))
        def _():
            w = GRID_W
            lane = lax.broadcasted_iota(jnp.int32, (w, 2 * w), 1)
            left = lane < w
            qc = lax.broadcasted_iota(jnp.int32, (w, 2 * w), 0)
            kc = lane & (w - 1)
            cstart = jnp.clip(qc - NA_COLS // 2, 0, w - NA_COLS)
            in_cols = (kc >= cstart) & (kc < cstart + NA_COLS)
            for hd, o in np.ndindex(N_HEADS, NA_MASKED_PLANE):
                vec = jnp.broadcast_to(delta_ref[hd, o:o + 1, :], (w, 2 * w))
                lo_half = pltpu.roll(vec, w + 1, axis=1, stride=1, stride_axis=0)
                hi_half = pltpu.roll(vec, 1, axis=1, stride=1, stride_axis=0)
                planes_ref[hd, o] = jnp.where(in_cols, jnp.where(left, lo_half, hi_half), MASK_VALUE)
            for hd in range(N_HEADS):
                planes_ref[hd, NA_MASKED_PLANE] = jnp.full((w, 2 * w), MASK_VALUE, jnp.float32)
            for c, qi, kp in np.ndindex(3, NA_QROWS, NA_WROWS // 2):
                pa, pb = (int(v) for v in _NA_PLANE[c, qi, 2 * kp:2 * kp + 2])
                for hd in range(N_HEADS):
                    blk2 = planes_ref[hd, pa] if pa == pb else jnp.where(left, planes_ref[hd, pa],
                                                                         planes_ref[hd, pb])
                    bias_ref[c, hd, qi * GRID_W:(qi + 1) * GRID_W,
                             2 * kp * GRID_W:2 * (kp + 1) * GRID_W] = blk2

    else:
        q_ref, kc_ref, vc_ref, o_ref = refs
    kc = kc_ref[0]
    vc = vc_ref[0]
    lane = lax.broadcasted_iota(jnp.int32, (NA_TQ, q_ref.shape[2]), 1)
    c2 = NA_SCALE * LOG2E
    n_blocks = q_ref.shape[1] // NA_TQ
    for sb in range(n_blocks):
        rows = slice(sb * NA_TQ, (sb + 1) * NA_TQ)
        if has_local:
            blk = pl.program_id(1) * n_blocks + sb
            nblk = pl.num_programs(1) * n_blocks
            wrow = jnp.clip(blk * NA_QROWS - NA_ROWS // 2, 0, GRID_W - NA_WROWS)
            start = pl.multiple_of(wrow * GRID_W, GRID_W)
            case = jnp.where(blk == 0, 0, jnp.where(blk == nblk - 1, 2, 1))
            kwin = k_ref[0, pl.ds(start, NA_TK), :]
            vwin = v_ref[0, pl.ds(start, NA_TK), :]
        q = q_ref[0, rows, :]
        out = jnp.zeros(q.shape, jnp.float32)
        for hd in range(N_HEADS):
            in_head = (lane >= hd * HEAD_DIM) & (lane < (hd + 1) * HEAD_DIM)
            qm = jnp.where(in_head, q, jnp.zeros_like(q))
            s_c = _dot_nt(qm, kc) * c2
            m = jnp.max(s_c, axis=-1, keepdims=True)
            if has_local:
                s_l = _dot_nt(qm, kwin) * c2 + bias_ref[case, hd]
                m = jnp.maximum(m, jnp.max(s_l, axis=-1, keepdims=True))
            p_c = jnp.exp2(s_c - m)
            den = jnp.sum(p_c, axis=-1, keepdims=True)
            o = _dot(p_c.astype(jnp.bfloat16), vc)
            if has_local:
                p_l = jnp.exp2(s_l - m)
                den = den + jnp.sum(p_l, axis=-1, keepdims=True)
                o = o + _dot(p_l.astype(jnp.bfloat16), vwin)
            out = jnp.where(in_head, o * (1.0 / den), out)
        o_ref[0, rows, :] = out.astype(jnp.bfloat16)


def _na(q, k, v, k_ctx, v_ctx, delta, layer):
    b, l, w = q.shape
    lc = k_ctx.shape[1]
    has_local = k is not None
    tq = NA_TQ * NA_BLOCKS_PER_STEP if has_local else NA_TQ
    assert l % tq == 0 and (not has_local or l == GRID_W * GRID_W), (l, tq)
    grid = (b, l // tq)
    qspec = pl.BlockSpec((1, tq, w), lambda i, j: (i, j, 0))
    full = lambda n: pl.BlockSpec((1, n, w), lambda i, j: (i, 0, 0))
    in_specs = [qspec]
    args = [q]
    if has_local:
        in_specs += [full(l), full(l)]
        args += [k, v]
    in_specs += [full(lc), full(lc)]
    args += [k_ctx, v_ctx]
    scratch = []
    if has_local:
        in_specs += [_layer_spec(delta, layer)]
        args += [delta]
        scratch = [pltpu.VMEM((3, N_HEADS, NA_TQ, NA_TK), jnp.float32),
                   pltpu.VMEM((N_HEADS, NA_MASKED_PLANE + 1, GRID_W, 2 * GRID_W), jnp.float32)]
    return pl.pallas_call(
        functools.partial(_na_kernel, has_local=has_local),
        grid=grid,
        in_specs=in_specs,
        out_specs=qspec,
        out_shape=jax.ShapeDtypeStruct((b, l, w), jnp.bfloat16),
        scratch_shapes=scratch,
        compiler_params=_cparams(2), name="na_attn",
    )(*args)


NA_MASKED_PLANE = 2 * NA_ROWS - 1


def _na_plane_index():
    rows = GRID_W
    idx = np.full((3, NA_QROWS, NA_WROWS), NA_MASKED_PLANE, np.int32)
    for c, r0 in enumerate((0, NA_QROWS, rows - NA_QROWS)):
        wrow = min(max(r0 - NA_ROWS // 2, 0), rows - NA_WROWS)
        for qi in range(NA_QROWS):
            r = r0 + qi
            band = min(max(r - NA_ROWS // 2, 0), rows - NA_ROWS)
            for kj in range(NA_WROWS):
                kr = wrow + kj
                if band <= kr < band + NA_ROWS:
                    idx[c, qi, kj] = kr - r + (NA_ROWS - 1)
    return idx


_NA_PLANE = _na_plane_index()


def _na_delta_table(rpb):
    w = GRID_W
    delta = np.arange(2 * w) - (w - 1)
    return rpb[..., np.clip(delta + (NA_COLS - 1), 0, 2 * NA_COLS - 2)] * LOG2E


def _outmlp_kernel(x_ref, yloc_ref, ymla_ref, yna_ref, mod_ref, gpm_ref, gpre_ref, gpost_ref,
                   wout_ref, w1_ref, w2_ref, o_ref, *, layer):
    ga1 = mod_ref[2:3, :]
    sh2 = mod_ref[3:4, :]
    sc2 = mod_ref[4:5, :]
    ga2 = mod_ref[5:6, :]
    this = slice(layer, layer + 1)
    g_mix = gpm_ref[this, :] * ga1
    g_mlp_in = gpre_ref[this, :] * (1.0 + sc2)
    g_mlp_out = gpost_ref[this, :] * ga2
    x = x_ref[0]
    ycat = jnp.concatenate([yloc_ref[0, :, 0:GROUP_W], ymla_ref[0], yloc_ref[0, :, GROUP_W:2 * GROUP_W],
                            yna_ref[0]], axis=-1)
    y = _dot(ycat, wout_ref[...])
    x1 = x + _rms(y) * g_mix
    hm = (_rms(x1) * g_mlp_in + sh2).astype(jnp.bfloat16)
    acc = jnp.zeros(x.shape, jnp.float32)
    for c in range(MLP_HIDDEN // MLP_CHUNK):
        lo = c * MLP_CHUNK
        hj = jnp.maximum(_dot(hm, w1_ref[:, lo:lo + MLP_CHUNK]), 0.0)
        acc = acc + _dot((hj * hj).astype(jnp.bfloat16), w2_ref[lo:lo + MLP_CHUNK, :])
    o_ref[0] = x1 + _rms(acc) * g_mlp_out


def _outmlp(x, yloc, ymla, yna, mods, mod_row, wts, layer, tm):
    b, l, d = x.shape
    assert l % tm == 0, (l, tm)
    grid = (b, l // tm)
    tok = lambda w: pl.BlockSpec((1, tm, w), lambda i, j: (i, j, 0))
    gains = ["g_post_mix", "g_pre_mlp", "g_post_mlp"]
    mats = ["w_out", "w_mlp1", "w_mlp2"]
    return pl.pallas_call(
        functools.partial(_outmlp_kernel, layer=layer),
        grid=grid,
        in_specs=[tok(d), tok(2 * GROUP_W), tok(GROUP_W), tok(GROUP_W), _mod_spec(mods, layer, mod_row)]
                 + [_layer_spec(wts[n], layer) for n in gains]
                 + [_layer_spec(wts[n], layer, single_buffer=True) for n in mats],
        out_specs=tok(d),
        out_shape=jax.ShapeDtypeStruct((b, l, d), jnp.float32),
        compiler_params=_cparams(2), name="outproj_mlp",
    )(x, yloc, ymla, yna, mods, *[wts[n] for n in gains], *[wts[n] for n in mats])


def _swap_halves_signed(w):
    q = MLA_ROPE // 4
    x1, x2, x3, x4 = (w[..., i * q:(i + 1) * q] for i in range(4))
    return jnp.concatenate([-x2, x1, -x4, x3], axis=-1)


def _prep_weights(p):
    bf16 = jnp.bfloat16
    w_in = p["w_in"]
    depth, d, _ = w_in.shape
    k_rope = w_in[..., MLA_Q_END + MLA_KV_RANK:MLA_KV_END]
    z32 = jnp.zeros((depth, d, MLA_ROPE), w_in.dtype)
    w_in_sc = w_in[..., :SC_END].astype(bf16)
    w_in_mla = jnp.concatenate([w_in[..., SC_END:MLA_Q_END + MLA_KV_RANK],
                                k_rope, z32, _swap_halves_signed(k_rope), z32], axis=-1).astype(bf16)
    w_in_cf = w_in[..., MLA_KV_END:CF_END].astype(bf16)
    w_in_na = w_in[..., CF_END:].astype(bf16)

    wq = p["w_uq"].reshape(depth, MLA_Q_RANK, N_HEADS, MLA_NOPE + MLA_ROPE)
    zq = jnp.zeros((depth, MLA_Q_RANK, N_HEADS, HEAD_PAD - MLA_NOPE - MLA_ROPE), wq.dtype)
    zn = jnp.zeros((depth, MLA_Q_RANK, N_HEADS, MLA_NOPE), wq.dtype)
    q_main = jnp.concatenate([wq, zq], axis=-1).reshape(depth, MLA_Q_RANK, MLA_PAD)
    q_swap = jnp.concatenate([zn, _swap_halves_signed(wq[..., MLA_NOPE:]), zq], axis=-1)
    w_uq_p = jnp.concatenate([q_main, q_swap.reshape(depth, MLA_Q_RANK, MLA_PAD)], axis=-1).astype(bf16)

    wkv = p["w_ukv"].reshape(depth, MLA_KV_RANK, N_HEADS, MLA_NOPE + MLA_V)
    zk = jnp.zeros((depth, MLA_KV_RANK, N_HEADS, HEAD_PAD - MLA_NOPE), wkv.dtype)
    k_main = jnp.concatenate([wkv[..., :MLA_NOPE], zk], axis=-1).reshape(depth, MLA_KV_RANK, MLA_PAD)
    v_main = jnp.concatenate([wkv[..., MLA_NOPE:], zk], axis=-1).reshape(depth, MLA_KV_RANK, MLA_PAD)
    w_ukv_p = jnp.concatenate([k_main, v_main], axis=-1).astype(bf16)

    out = {name: p[name] for name in ("g_pre_mix", "g_q", "g_kv", "w_sc", "cf_w_dw", "cf_b_dw", "cf_ln_g",
                                      "cf_ln_b", "g_post_mix", "g_pre_mlp", "g_post_mlp")}
    out.update({
        "w_in_sc": w_in_sc, "w_in_mla": w_in_mla, "w_in_cf": w_in_cf, "w_in_na": w_in_na,
        "w_uq": w_uq_p, "w_ukv": w_ukv_p, "cf_w_pw": p["cf_w_pw"].astype(bf16),
        "w_out": p["w_out"].astype(bf16), "w_mlp1": p["w_mlp1"].astype(bf16),
        "w_mlp2": p["w_mlp2"].astype(bf16), "na_delta": _na_delta_table(p["na_rpb"]),
    })
    return out


def _rope_tables(l, rotate):
    f32 = np.float32
    ones = np.ones((l, MLA_NOPE), f32)
    zpad = np.zeros((l, HEAD_PAD - MLA_NOPE - MLA_ROPE), f32)
    z32 = np.zeros((l, MLA_ROPE), f32)
    if rotate:
        pos = np.arange(l)
        n_freq = MLA_ROPE // 4
        inv_freq = (f32(ROPE_BASE) ** (-np.arange(n_freq, dtype=f32) / f32(n_freq))).astype(f32)
        ang_row = (pos // GRID_W).astype(f32)[:, None] * inv_freq
        ang_col = (pos % GRID_W).astype(f32)[:, None] * inv_freq
        cos = np.concatenate([np.cos(ang_row)] * 2 + [np.cos(ang_col)] * 2, axis=-1).astype(f32)
        sin = np.concatenate([np.sin(ang_row)] * 2 + [np.sin(ang_col)] * 2, axis=-1).astype(f32)
    else:
        cos = np.ones((l, MLA_ROPE), f32)
        sin = z32
    return {"cq": jnp.asarray(np.concatenate([ones, cos, zpad], axis=-1)),
            "sq": jnp.asarray(np.concatenate([np.zeros_like(ones), sin, zpad], axis=-1)),
            "tk": jnp.asarray(np.concatenate([cos, z32, sin, z32], axis=-1))}


def kernel(x, c, ctx, c_ctx, w_mod, b_mod, g_pre_mix, w_in, w_sc, g_q, w_uq, g_kv, w_ukv, cf_w_dw,
           cf_b_dw, cf_ln_g, cf_ln_b, cf_w_pw, na_rpb, w_out, g_post_mix, g_pre_mlp, w_mlp1, w_mlp2,
           g_post_mlp):
    b, l, d = x.shape
    lc = ctx.shape[1]
    depth = w_mod.shape[0]
    wts = _prep_weights({
        "g_pre_mix": g_pre_mix, "w_in": w_in, "w_sc": w_sc, "g_q": g_q, "w_uq": w_uq, "g_kv": g_kv,
        "w_ukv": w_ukv, "cf_w_dw": cf_w_dw, "cf_b_dw": cf_b_dw, "cf_ln_g": cf_ln_g, "cf_ln_b": cf_ln_b,
        "cf_w_pw": cf_w_pw, "na_rpb": na_rpb, "w_out": w_out, "g_post_mix": g_post_mix,
        "g_pre_mlp": g_pre_mlp, "w_mlp1": w_mlp1, "w_mlp2": w_mlp2, "g_post_mlp": g_post_mlp})

    n_rows = SUBLANES
    assert b + 1 <= n_rows and d == D_MODEL and l == GRID_W * GRID_W, (b, l, d)
    cs = jnp.concatenate([c, c_ctx[None, :], jnp.zeros((n_rows - b - 1, d), c.dtype)], axis=0)
    mods = _modulation(cs, w_mod, b_mod).reshape(depth, n_rows, N_MOD, d)
    ctx_row = b
    tabs_lat = _rope_tables(l, True)
    tabs_ctx = _rope_tables(lc, False)

    tm = 1024
    tq = 1024
    xc = ctx
    for i in range(depth):
        last = i == depth - 1
        y_loc, q, k, v, nq, nk, nv = _inproj(x, mods, None, wts, i, tabs_lat, tm)
        yc_loc, q_c, k_c, v_c, nq_c, nk_c, nv_c = _inproj(xc, mods, ctx_row, wts, i, tabs_ctx, lc)

        y_mla = _mla(q, k, v, k_c, v_c, tq)
        y_na = _na(nq, nk, nv, nk_c, nv_c, wts["na_delta"], i)
        x = _outmlp(x, y_loc, y_mla, y_na, mods, None, wts, i, tm)

        if not last:
            yc_mla = _mla(q_c, None, None, k_c, v_c, lc)
            yc_na = _na(nq_c, None, None, nk_c, nv_c, None, i)
            xc = _outmlp(xc, yc_loc, yc_mla, yc_na, mods, ctx_row, wts, i, lc)
    return x
```

```python
import functools

import jax
import jax.numpy as jnp
import numpy as np
from jax import lax
from jax.experimental import pallas as pl
from jax.experimental.pallas import tpu as pltpu

D_MODEL = 1024
GRID_W = 64
GROUP_W = D_MODEL // 4
HEAD_DIM = 64
N_HEADS = GROUP_W // HEAD_DIM
SC_KERNEL = 3
MLA_Q_RANK = 256
MLA_KV_RANK = 128
MLA_NOPE = 64
MLA_ROPE = 32
MLA_V = 64
MLA_SCALE = (MLA_NOPE + MLA_ROPE) ** -0.5
CF_KERNEL = 31
NA_ROWS = 8
NA_COLS = 16
NA_SCALE = HEAD_DIM ** -0.5
ROPE_BASE = 10000.0
MLP_HIDDEN = 4 * D_MODEL
N_MOD = 6
EPS = 1e-6

SC_END = 3 * GROUP_W
MLA_Q_END = SC_END + MLA_Q_RANK
MLA_KV_END = MLA_Q_END + MLA_KV_RANK + MLA_ROPE
CF_END = MLA_KV_END + 2 * GROUP_W
NA_Q_END = CF_END + GROUP_W
P_IN = NA_Q_END + 2 * GROUP_W

LANES = 128
SUBLANES = 8
HEAD_PAD = LANES
MLA_PAD = N_HEADS * HEAD_PAD

NA_QROWS = 4
NA_WROWS = NA_QROWS + NA_ROWS
NA_TQ = NA_QROWS * GRID_W
NA_TK = NA_WROWS * GRID_W
NA_BLOCKS_PER_STEP = 4
HALO = 16
CONV_ROWS = 128
INPROJ_SUB = 512
MASK_VALUE = -1e30
MLA_KCHUNK = 2048
MLA_HEADS_PER_STEP = 4
LOG2E = 1.4426950408889634
MLP_CHUNK = 1024
MOD_BLOCK = 2048

VMEM_LIMIT = 56 * 1024 * 1024


def _cparams(n_axes):
    return pltpu.CompilerParams(dimension_semantics=("arbitrary",) * n_axes,
                                vmem_limit_bytes=VMEM_LIMIT)


def _layer_spec(arr, layer, single_buffer=False):
    nd = arr.ndim - 1
    if nd == 1:
        return pl.BlockSpec(arr.shape, lambda *_: (0, 0))
    kwargs = {"pipeline_mode": pl.Buffered(1)} if single_buffer else {}
    return pl.BlockSpec((None,) + arr.shape[1:], lambda *_: (layer,) + (0,) * nd, **kwargs)


def _mod_spec(mods, layer, row):
    if row is None:
        return pl.BlockSpec((None, None) + mods.shape[2:], lambda i, j: (layer, i, 0, 0))
    return pl.BlockSpec((None, None) + mods.shape[2:], lambda i, j: (layer, row, 0, 0))


def _rms(x):
    return x * lax.rsqrt(jnp.mean(x * x, axis=-1, keepdims=True) + EPS)


def _dot(a, b):
    return jnp.dot(a, b, preferred_element_type=jnp.float32)


def _dot_nt(a, b):
    return lax.dot_general(a, b, (((1,), (1,)), ((), ())), preferred_element_type=jnp.float32)


def _mod_kernel(cs_ref, w_ref, b_ref, o_ref):
    cs = cs_ref[...]
    a = (cs * jax.nn.sigmoid(cs)).astype(jnp.bfloat16)
    o_ref[0] = _dot(a, w_ref[0].astype(jnp.bfloat16)) + b_ref[0]


def _modulation(cs, w_mod, b_mod):
    depth, d, n = w_mod.shape
    bn = MOD_BLOCK
    return pl.pallas_call(
        _mod_kernel,
        grid=(depth, n // bn),
        in_specs=[pl.BlockSpec(cs.shape, lambda l, j: (0, 0)),
                  pl.BlockSpec((1, d, bn), lambda l, j: (l, 0, j)),
                  pl.BlockSpec((1, 1, bn), lambda l, j: (l, 0, j))],
        out_specs=pl.BlockSpec((1, cs.shape[0], bn), lambda l, j: (l, 0, j)),
        out_shape=jax.ShapeDtypeStruct((depth, cs.shape[0], n), jnp.float32),
        compiler_params=_cparams(2), name="modulation",
    )(cs, w_mod, b_mod.reshape(depth, 1, n))


def _inproj_kernel(x_ref, xp_ref, xn_ref, mod_ref, gpre_ref, wsc_in_ref, wmla_in_ref, wcf_in_ref, wna_in_ref,
                   gq_ref, wuq_ref, gkv_ref, wukv_ref,
                   cq_ref, sq_ref, tk_ref, wsc_ref, wdw_ref, bdw_ref, lng_ref, lnb_ref, wpw_ref,
                   yloc_ref, q_ref, k_ref, v_ref, nq_ref, nk_ref, nv_ref,
                   h_ext, ext_sc, ext_cf, shifted, conv_out, *, tm, sub, layer):
    j = pl.program_id(1)
    this = slice(layer, layer + 1)
    sh1 = mod_ref[0:1, :]
    sc1 = mod_ref[1:2, :]
    gain = gpre_ref[this, :] * (1.0 + sc1)

    def modulated(x):
        return (_rms(x) * gain + sh1).astype(jnp.bfloat16)

    n_sub = tm // sub
    n_ext = sub + 2 * HALO
    for t in range(n_sub):
        r0 = t * sub
        rows = slice(r0, r0 + sub)
        x_prev = xp_ref[0] if t == 0 else x_ref[0, r0 - HALO:r0, :]
        x_next = xn_ref[0] if t == n_sub - 1 else x_ref[0, r0 + sub:r0 + sub + HALO, :]
        h_ext[t, 0:HALO, :] = modulated(x_prev)
        h_ext[t, HALO:HALO + sub, :] = modulated(x_ref[0, rows, :])
        h_ext[t, HALO + sub:n_ext, :] = modulated(x_next)
        h_all = h_ext[t]
        h_main = h_ext[t, HALO:HALO + sub, :]
        u_cf = _dot(h_all, wcf_in_ref[...])
        u_sc = _dot(h_all, wsc_in_ref[...])
        u_mla = _dot(h_main, wmla_in_ref[...])
        u_na = _dot(h_main, wna_in_ref[...])

        row = lax.broadcasted_iota(jnp.int32, (n_ext, 1), 0)
        lo_row = jnp.where(j > 0, 0, HALO) if t == 0 else 0
        hi_row = jnp.where(j < pl.num_programs(1) - 1, n_ext, sub + HALO) if t == n_sub - 1 else n_ext
        in_seq = (row >= lo_row) & (row < hi_row)
        ext_sc[t] = jnp.where(in_seq, u_sc[:, 2 * GROUP_W:3 * GROUP_W] * u_sc[:, 0:GROUP_W], 0.0)
        ext_cf[t] = jnp.where(in_seq, u_cf[:, 0:GROUP_W] * jax.nn.sigmoid(u_cf[:, GROUP_W:2 * GROUP_W]), 0.0)

        acc = jnp.zeros((sub, GROUP_W), jnp.float32)
        for k in range(SC_KERNEL):
            off = HALO - SC_KERNEL // 2 + k
            acc = acc + ext_sc[t, off:off + sub, :] * wsc_ref[k:k + 1, :]
        y_sc = u_sc[HALO:HALO + sub, GROUP_W:2 * GROUP_W] * acc

        base = HALO - CF_KERNEL // 2
        s_len = shifted.shape[2]
        for r in range(1, SUBLANES):
            shifted[t, r] = ext_cf[t, r:r + s_len, :]
        for c0 in range(0, sub, CONV_ROWS):
            acc = jnp.zeros((CONV_ROWS, GROUP_W), jnp.float32)
            for k in range(CF_KERNEL):
                r, lo = (base + k) % SUBLANES, (base + k) // SUBLANES * SUBLANES + c0
                src = ext_cf[t, lo:lo + CONV_ROWS, :] if r == 0 else shifted[t, r, lo:lo + CONV_ROWS, :]
                acc = acc + src * wdw_ref[k:k + 1, :]
            conv_out[t, c0:c0 + CONV_ROWS, :] = acc
        y = conv_out[t] + bdw_ref[this, :]
        mu = jnp.mean(y, axis=-1, keepdims=True)
        yc = y - mu
        var = jnp.mean(yc * yc, axis=-1, keepdims=True)
        z = (yc * lax.rsqrt(var + EPS)) * lng_ref[this, :] + lnb_ref[this, :]
        z = z * jax.nn.sigmoid(z)
        y_cf = _dot(z.astype(jnp.bfloat16), wpw_ref[...])
        yloc_ref[0, rows, 0:GROUP_W] = y_sc.astype(jnp.bfloat16)
        yloc_ref[0, rows, GROUP_W:2 * GROUP_W] = y_cf.astype(jnp.bfloat16)

        nq_ref[0, rows, :] = u_na[:, 0:GROUP_W].astype(jnp.bfloat16)
        nk_ref[0, rows, :] = u_na[:, GROUP_W:2 * GROUP_W].astype(jnp.bfloat16)
        nv_ref[0, rows, :] = u_na[:, 2 * GROUP_W:3 * GROUP_W].astype(jnp.bfloat16)

        cqn = (_rms(u_mla[:, 0:MLA_Q_RANK]) * gq_ref[this, :]).astype(jnp.bfloat16)
        qf = _dot(cqn, wuq_ref[...])
        cq_t = cq_ref[rows, :]
        sq_t = sq_ref[rows, :]
        for hd in range(N_HEADS):
            lo = hd * HEAD_PAD
            qh = qf[:, lo:lo + HEAD_PAD] * cq_t + qf[:, MLA_PAD + lo:MLA_PAD + lo + HEAD_PAD] * sq_t
            q_ref[0, rows, lo:lo + HEAD_PAD] = qh.astype(jnp.bfloat16)

        ckv_lo = MLA_Q_RANK
        rope_lo = MLA_Q_RANK + MLA_KV_RANK
        ckvn = (_rms(u_mla[:, ckv_lo:rope_lo]) * gkv_ref[this, :]).astype(jnp.bfloat16)
        kvf = _dot(ckvn, wukv_ref[...])
        g = u_mla[:, rope_lo:rope_lo + LANES] * tk_ref[rows, :]
        lane = lax.broadcasted_iota(jnp.int32, g.shape, 1)
        rot = pltpu.roll(g, 2 * MLA_ROPE, axis=1) + g
        kr = jnp.where((lane >= MLA_NOPE) & (lane < MLA_NOPE + MLA_ROPE), rot, 0.0)
        for hd in range(N_HEADS):
            lo = hd * HEAD_PAD
            k_ref[0, rows, lo:lo + HEAD_PAD] = (kvf[:, lo:lo + HEAD_PAD] + kr).astype(jnp.bfloat16)
            vh = kvf[:, MLA_PAD + lo:MLA_PAD + lo + HEAD_PAD]
            v_ref[0, rows, lo:lo + HEAD_PAD] = jnp.where(lane == MLA_V, 1.0, vh).astype(jnp.bfloat16)


def _inproj(x, mods, mod_row, wts, layer, tabs, tm):
    b, l, d = x.shape
    grid = (b, l // tm)
    hb = tm // HALO
    nhb = l // HALO
    tok = lambda w: pl.BlockSpec((1, tm, w), lambda i, j: (i, j, 0))
    prev = pl.BlockSpec((1, HALO, d), lambda i, j: (i, jnp.maximum(j * hb - 1, 0), 0))
    nxt = pl.BlockSpec((1, HALO, d), lambda i, j: (i, jnp.minimum((j + 1) * hb, nhb - 1), 0))
    tab = pl.BlockSpec((tm, LANES), lambda i, j: (j, 0))
    f32, bf16 = jnp.float32, jnp.bfloat16
    widths = [2 * GROUP_W, MLA_PAD, MLA_PAD, MLA_PAD, GROUP_W, GROUP_W, GROUP_W]
    names = ["g_pre_mix", "w_in_sc", "w_in_mla", "w_in_cf", "w_in_na", "g_q", "w_uq", "g_kv", "w_ukv"]
    conv_names = ["w_sc", "cf_w_dw", "cf_b_dw", "cf_ln_g", "cf_ln_b", "cf_w_pw"]
    sub = min(tm, INPROJ_SUB)
    n_sub = tm // sub
    assert l % tm == 0 and tm % sub == 0 and sub % CONV_ROWS == 0 and tm % HALO == 0, (l, tm, sub)
    s_len = (HALO - CF_KERNEL // 2 + CF_KERNEL - 1) // SUBLANES * SUBLANES + sub
    return pl.pallas_call(
        functools.partial(_inproj_kernel, tm=tm, sub=sub, layer=layer),
        grid=grid,
        in_specs=[tok(d), prev, nxt, _mod_spec(mods, layer, mod_row)]
                 + [_layer_spec(wts[n], layer) for n in names] + [tab, tab, tab]
                 + [_layer_spec(wts[n], layer) for n in conv_names],
        out_specs=[tok(w) for w in widths],
        out_shape=[jax.ShapeDtypeStruct((b, l, w), bf16) for w in widths],
        scratch_shapes=[pltpu.VMEM((n_sub, sub + 2 * HALO, d), bf16),
                        pltpu.VMEM((n_sub, sub + 2 * HALO, GROUP_W), f32),
                        pltpu.VMEM((n_sub, sub + 2 * HALO, GROUP_W), f32),
                        pltpu.VMEM((n_sub, SUBLANES, s_len, GROUP_W), f32),
                        pltpu.VMEM((n_sub, sub, GROUP_W), f32)],
        compiler_params=_cparams(2), name="inproj",
    )(x, x, x, mods, *[wts[n] for n in names], tabs["cq"], tabs["sq"], tabs["tk"],
      *[wts[n] for n in conv_names])


def _mla_kernel(*refs, has_lat):
    if has_lat:
        q_ref, kl_ref, vl_ref, kc_ref, vc_ref, o_ref = refs
    else:
        q_ref, kc_ref, vc_ref, o_ref = refs
    chunks = [(kc_ref, vc_ref, 0, kc_ref.shape[1])]
    if has_lat:
        n_lat = kl_ref.shape[1]
        chunks += [(kl_ref, vl_ref, lo, MLA_KCHUNK) for lo in range(0, n_lat, MLA_KCHUNK)]
    c2 = MLA_SCALE * LOG2E
    outs = []
    for hd in range(MLA_HEADS_PER_STEP):
        lanes = slice(hd * HEAD_PAD, (hd + 1) * HEAD_PAD)
        q = q_ref[0, :, lanes]
        m = o = None
        for k_ref, v_ref, lo, n in chunks:
            s = _dot_nt(q, k_ref[0, lo:lo + n, lanes])
            m_chunk = jnp.max(s, axis=-1, keepdims=True)
            m_new = m_chunk if m is None else jnp.maximum(m, m_chunk)
            p = jnp.exp2((s - m_new) * c2).astype(jnp.bfloat16)
            pv = _dot(p, v_ref[0, lo:lo + n, lanes])
            o = pv if o is None else o * jnp.exp2((m - m_new) * c2) + pv
            m = m_new
        outs.append(o * (1.0 / o[:, MLA_V:MLA_V + 1]))
    lane = lax.broadcasted_iota(jnp.int32, outs[0].shape, 1)
    for pair in range(MLA_HEADS_PER_STEP // 2):
        packed = jnp.where(lane < MLA_V, outs[2 * pair], pltpu.roll(outs[2 * pair + 1], MLA_V, axis=1))
        o_ref[0, :, pair * HEAD_PAD:(pair + 1) * HEAD_PAD] = packed.astype(jnp.bfloat16)


def _mla(q, k_lat, v_lat, k_ctx, v_ctx, tq):
    b, lq, _ = q.shape
    lc = k_ctx.shape[1]
    has_lat = k_lat is not None
    hw = MLA_HEADS_PER_STEP * HEAD_PAD
    assert lq % tq == 0 and (not has_lat or k_lat.shape[1] % MLA_KCHUNK == 0), (lq, tq)
    grid = (b, N_HEADS // MLA_HEADS_PER_STEP, lq // tq)
    qspec = pl.BlockSpec((1, tq, hw), lambda i, h, j: (i, j, h))
    kvspec = lambda n: pl.BlockSpec((1, n, hw), lambda i, h, j: (i, 0, h))
    in_specs = [qspec]
    args = [q]
    if has_lat:
        in_specs += [kvspec(k_lat.shape[1])] * 2
        args += [k_lat, v_lat]
    in_specs += [kvspec(lc)] * 2
    args += [k_ctx, v_ctx]
    return pl.pallas_call(
        functools.partial(_mla_kernel, has_lat=has_lat),
        grid=grid,
        in_specs=in_specs,
        out_specs=pl.BlockSpec((1, tq, MLA_HEADS_PER_STEP * MLA_V), lambda i, h, j: (i, j, h)),
        out_shape=jax.ShapeDtypeStruct((b, lq, N_HEADS * MLA_V), jnp.bfloat16),
        compiler_params=_cparams(3), name="mla_attn",
    )(*args)


def _na_kernel(*refs, has_local):
    if has_local:
        q_ref, k_ref, v_ref, kc_ref, vc_ref, delta_ref, o_ref, bias_ref, planes_ref = refs

        @pl.when((pl.program_id(0) == 0) & (pl.program_id(1) == 0))
        def _():
            w = GRID_W
            lane = lax.broadcasted_iota(jnp.int32, (w, 2 * w), 1)
            left = lane < w
            qc = lax.broadcasted_iota(jnp.int32, (w, 2 * w), 0)
            kc = lane & (w - 1)
            cstart = jnp.clip(qc - NA_COLS // 2, 0, w - NA_COLS)
            in_cols = (kc >= cstart) & (kc < cstart + NA_COLS)
            for hd, o in np.ndindex(N_HEADS, NA_MASKED_PLANE):
                vec = jnp.broadcast_to(delta_ref[hd, o:o + 1, :], (w, 2 * w))
                lo_half = pltpu.roll(vec, w + 1, axis=1, stride=1, stride_axis=0)
                hi_half = pltpu.roll(vec, 1, axis=1, stride=1, stride_axis=0)
                planes_ref[hd, o] = jnp.where(in_cols, jnp.where(left, lo_half, hi_half), MASK_VALUE)
            for hd in range(N_HEADS):
                planes_ref[hd, NA_MASKED_PLANE] = jnp.full((w, 2 * w), MASK_VALUE, jnp.float32)
            for c, qi, kp in np.ndindex(3, NA_QROWS, NA_WROWS // 2):
                pa, pb = (int(v) for v in _NA_PLANE[c, qi, 2 * kp:2 * kp + 2])
                for hd in range(N_HEADS):
                    blk2 = planes_ref[hd, pa] if pa == pb else jnp.where(left, planes_ref[hd, pa],
                                                                         planes_ref[hd, pb])
                    bias_ref[c, hd, qi * GRID_W:(qi + 1) * GRID_W,
                             2 * kp * GRID_W:2 * (kp + 1) * GRID_W] = blk2

    else:
        q_ref, kc_ref, vc_ref, o_ref = refs
    kc = kc_ref[0]
    vc = vc_ref[0]
    lane = lax.broadcasted_iota(jnp.int32, (NA_TQ, q_ref.shape[2]), 1)
    c2 = NA_SCALE * LOG2E
    n_blocks = q_ref.shape[1] // NA_TQ
    for sb in range(n_blocks):
        rows = slice(sb * NA_TQ, (sb + 1) * NA_TQ)
        if has_local:
            blk = pl.program_id(1) * n_blocks + sb
            nblk = pl.num_programs(1) * n_blocks
            wrow = jnp.clip(blk * NA_QROWS - NA_ROWS // 2, 0, GRID_W - NA_WROWS)
            start = pl.multiple_of(wrow * GRID_W, GRID_W)
            case = jnp.where(blk == 0, 0, jnp.where(blk == nblk - 1, 2, 1))
            kwin = k_ref[0, pl.ds(start, NA_TK), :]
            vwin = v_ref[0, pl.ds(start, NA_TK), :]
        q = q_ref[0, rows, :]
        out = jnp.zeros(q.shape, jnp.float32)
        for hd in range(N_HEADS):
            in_head = (lane >= hd * HEAD_DIM) & (lane < (hd + 1) * HEAD_DIM)
            qm = jnp.where(in_head, q, jnp.zeros_like(q))
            s_c = _dot_nt(qm, kc) * c2
            m = jnp.max(s_c, axis=-1, keepdims=True)
            if has_local:
                s_l = _dot_nt(qm, kwin) * c2 + bias_ref[case, hd]
                m = jnp.maximum(m, jnp.max(s_l, axis=-1, keepdims=True))
            p_c = jnp.exp2(s_c - m)
            den = jnp.sum(p_c, axis=-1, keepdims=True)
            o = _dot(p_c.astype(jnp.bfloat16), vc)
            if has_local:
                p_l = jnp.exp2(s_l - m)
                den = den + jnp.sum(p_l, axis=-1, keepdims=True)
                o = o + _dot(p_l.astype(jnp.bfloat16), vwin)
            out = jnp.where(in_head, o * (1.0 / den), out)
        o_ref[0, rows, :] = out.astype(jnp.bfloat16)


def _na(q, k, v, k_ctx, v_ctx, delta, layer):
    b, l, w = q.shape
    lc = k_ctx.shape[1]
    has_local = k is not None
    tq = NA_TQ * NA_BLOCKS_PER_STEP if has_local else NA_TQ
    assert l % tq == 0 and (not has_local or l == GRID_W * GRID_W), (l, tq)
    grid = (b, l // tq)
    qspec = pl.BlockSpec((1, tq, w), lambda i, j: (i, j, 0))
    full = lambda n: pl.BlockSpec((1, n, w), lambda i, j: (i, 0, 0))
    in_specs = [qspec]
    args = [q]
    if has_local:
        in_specs += [full(l), full(l)]
        args += [k, v]
    in_specs += [full(lc), full(lc)]
    args += [k_ctx, v_ctx]
    scratch = []
    if has_local:
        in_specs += [_layer_spec(delta, layer)]
        args += [delta]
        scratch = [pltpu.VMEM((3, N_HEADS, NA_TQ, NA_TK), jnp.float32),
                   pltpu.VMEM((N_HEADS, NA_MASKED_PLANE + 1, GRID_W, 2 * GRID_W), jnp.float32)]
    return pl.pallas_call(
        functools.partial(_na_kernel, has_local=has_local),
        grid=grid,
        in_specs=in_specs,
        out_specs=qspec,
        out_shape=jax.ShapeDtypeStruct((b, l, w), jnp.bfloat16),
        scratch_shapes=scratch,
        compiler_params=_cparams(2), name="na_attn",
    )(*args)


NA_MASKED_PLANE = 2 * NA_ROWS - 1


def _na_plane_index():
    rows = GRID_W
    idx = np.full((3, NA_QROWS, NA_WROWS), NA_MASKED_PLANE, np.int32)
    for c, r0 in enumerate((0, NA_QROWS, rows - NA_QROWS)):
        wrow = min(max(r0 - NA_ROWS // 2, 0), rows - NA_WROWS)
        for qi in range(NA_QROWS):
            r = r0 + qi
            band = min(max(r - NA_ROWS // 2, 0), rows - NA_ROWS)
            for kj in range(NA_WROWS):
                kr = wrow + kj
                if band <= kr < band + NA_ROWS:
                    idx[c, qi, kj] = kr - r + (NA_ROWS - 1)
    return idx


_NA_PLANE = _na_plane_index()


def _na_delta_table(rpb):
    w = GRID_W
    delta = np.arange(2 * w) - (w - 1)
    return rpb[..., np.clip(delta + (NA_COLS - 1), 0, 2 * NA_COLS - 2)] * LOG2E


def _outmlp_kernel(x_ref, yloc_ref, ymla_ref, yna_ref, mod_ref, gpm_ref, gpre_ref, gpost_ref,
                   wout_ref, w1_ref, w2_ref, o_ref, *, layer):
    ga1 = mod_ref[2:3, :]
    sh2 = mod_ref[3:4, :]
    sc2 = mod_ref[4:5, :]
    ga2 = mod_ref[5:6, :]
    this = slice(layer, layer + 1)
    g_mix = gpm_ref[this, :] * ga1
    g_mlp_in = gpre_ref[this, :] * (1.0 + sc2)
    g_mlp_out = gpost_ref[this, :] * ga2
    x = x_ref[0]
    ycat = jnp.concatenate([yloc_ref[0, :, 0:GROUP_W], ymla_ref[0], yloc_ref[0, :, GROUP_W:2 * GROUP_W],
                            yna_ref[0]], axis=-1)
    y = _dot(ycat, wout_ref[...])
    x1 = x + _rms(y) * g_mix
    hm = (_rms(x1) * g_mlp_in + sh2).astype(jnp.bfloat16)
    acc = jnp.zeros(x.shape, jnp.float32)
    for c in range(MLP_HIDDEN // MLP_CHUNK):
        lo = c * MLP_CHUNK
        hj = jnp.maximum(_dot(hm, w1_ref[:, lo:lo + MLP_CHUNK]), 0.0)
        acc = acc + _dot((hj * hj).astype(jnp.bfloat16), w2_ref[lo:lo + MLP_CHUNK, :])
    o_ref[0] = x1 + _rms(acc) * g_mlp_out


def _outmlp(x, yloc, ymla, yna, mods, mod_row, wts, layer, tm):
    b, l, d = x.shape
    assert l % tm == 0, (l, tm)
    grid = (b, l // tm)
    tok = lambda w: pl.BlockSpec((1, tm, w), lambda i, j: (i, j, 0))
    gains = ["g_post_mix", "g_pre_mlp", "g_post_mlp"]
    mats = ["w_out", "w_mlp1", "w_mlp2"]
    return pl.pallas_call(
        functools.partial(_outmlp_kernel, layer=layer),
        grid=grid,
        in_specs=[tok(d), tok(2 * GROUP_W), tok(GROUP_W), tok(GROUP_W), _mod_spec(mods, layer, mod_row)]
                 + [_layer_spec(wts[n], layer) for n in gains]
                 + [_layer_spec(wts[n], layer, single_buffer=True) for n in mats],
        out_specs=tok(d),
        out_shape=jax.ShapeDtypeStruct((b, l, d), jnp.float32),
        compiler_params=_cparams(2), name="outproj_mlp",
    )(x, yloc, ymla, yna, mods, *[wts[n] for n in gains], *[wts[n] for n in mats])


def _swap_halves_signed(w):
    q = MLA_ROPE // 4
    x1, x2, x3, x4 = (w[..., i * q:(i + 1) * q] for i in range(4))
    return jnp.concatenate([-x2, x1, -x4, x3], axis=-1)


def _prep_weights(p):
    bf16 = jnp.bfloat16
    w_in = p["w_in"]
    depth, d, _ = w_in.shape
    k_rope = w_in[..., MLA_Q_END + MLA_KV_RANK:MLA_KV_END]
    z32 = jnp.zeros((depth, d, MLA_ROPE), w_in.dtype)
    w_in_sc = w_in[..., :SC_END].astype(bf16)
    w_in_mla = jnp.concatenate([w_in[..., SC_END:MLA_Q_END + MLA_KV_RANK],
                                k_rope, z32, _swap_halves_signed(k_rope), z32], axis=-1).astype(bf16)
    w_in_cf = w_in[..., MLA_KV_END:CF_END].astype(bf16)
    w_in_na = w_in[..., CF_END:].astype(bf16)

    wq = p["w_uq"].reshape(depth, MLA_Q_RANK, N_HEADS, MLA_NOPE + MLA_ROPE)
    zq = jnp.zeros((depth, MLA_Q_RANK, N_HEADS, HEAD_PAD - MLA_NOPE - MLA_ROPE), wq.dtype)
    zn = jnp.zeros((depth, MLA_Q_RANK, N_HEADS, MLA_NOPE), wq.dtype)
    q_main = jnp.concatenate([wq, zq], axis=-1).reshape(depth, MLA_Q_RANK, MLA_PAD)
    q_swap = jnp.concatenate([zn, _swap_halves_signed(wq[..., MLA_NOPE:]), zq], axis=-1)
    w_uq_p = jnp.concatenate([q_main, q_swap.reshape(depth, MLA_Q_RANK, MLA_PAD)], axis=-1).astype(bf16)

    wkv = p["w_ukv"].reshape(depth, MLA_KV_RANK, N_HEADS, MLA_NOPE + MLA_V)
    zk = jnp.zeros((depth, MLA_KV_RANK, N_HEADS, HEAD_PAD - MLA_NOPE), wkv.dtype)
    k_main = jnp.concatenate([wkv[..., :MLA_NOPE], zk], axis=-1).reshape(depth, MLA_KV_RANK, MLA_PAD)
    v_main = jnp.concatenate([wkv[..., MLA_NOPE:], zk], axis=-1).reshape(depth, MLA_KV_RANK, MLA_PAD)
    w_ukv_p = jnp.concatenate([k_main, v_main], axis=-1).astype(bf16)

    out = {name: p[name] for name in ("g_pre_mix", "g_q", "g_kv", "w_sc", "cf_w_dw", "cf_b_dw", "cf_ln_g",
                                      "cf_ln_b", "g_post_mix", "g_pre_mlp", "g_post_mlp")}
    out.update({
        "w_in_sc": w_in_sc, "w_in_mla": w_in_mla, "w_in_cf": w_in_cf, "w_in_na": w_in_na,
        "w_uq": w_uq_p, "w_ukv": w_ukv_p, "cf_w_pw": p["cf_w_pw"].astype(bf16),
        "w_out": p["w_out"].astype(bf16), "w_mlp1": p["w_mlp1"].astype(bf16),
        "w_mlp2": p["w_mlp2"].astype(bf16), "na_delta": _na_delta_table(p["na_rpb"]),
    })
    return out


def _rope_tables(l, rotate):
    f32 = np.float32
    ones = np.ones((l, MLA_NOPE), f32)
    zpad = np.zeros((l, HEAD_PAD - MLA_NOPE - MLA_ROPE), f32)
    z32 = np.zeros((l, MLA_ROPE), f32)
    if rotate:
        pos = np.arange(l)
        n_freq = MLA_ROPE // 4
        inv_freq = (f32(ROPE_BASE) ** (-np.arange(n_freq, dtype=f32) / f32(n_freq))).astype(f32)
        ang_row = (pos // GRID_W).astype(f32)[:, None] * inv_freq
        ang_col = (pos % GRID_W).astype(f32)[:, None] * inv_freq
        cos = np.concatenate([np.cos(ang_row)] * 2 + [np.cos(ang_col)] * 2, axis=-1).astype(f32)
        sin = np.concatenate([np.sin(ang_row)] * 2 + [np.sin(ang_col)] * 2, axis=-1).astype(f32)
    else:
        cos = np.ones((l, MLA_ROPE), f32)
        sin = z32
    return {"cq": jnp.asarray(np.concatenate([ones, cos, zpad], axis=-1)),
            "sq": jnp.asarray(np.concatenate([np.zeros_like(ones), sin, zpad], axis=-1)),
            "tk": jnp.asarray(np.concatenate([cos, z32, sin, z32], axis=-1))}


def kernel(x, c, ctx, c_ctx, w_mod, b_mod, g_pre_mix, w_in, w_sc, g_q, w_uq, g_kv, w_ukv, cf_w_dw,
           cf_b_dw, cf_ln_g, cf_ln_b, cf_w_pw, na_rpb, w_out, g_post_mix, g_pre_mlp, w_mlp1, w_mlp2,
           g_post_mlp):
    b, l, d = x.shape
    lc = ctx.shape[1]
    depth = w_mod.shape[0]
    wts = _prep_weights({
        "g_pre_mix": g_pre_mix, "w_in": w_in, "w_sc": w_sc, "g_q": g_q, "w_uq": w_uq, "g_kv": g_kv,
        "w_ukv": w_ukv, "cf_w_dw": cf_w_dw, "cf_b_dw": cf_b_dw, "cf_ln_g": cf_ln_g, "cf_ln_b": cf_ln_b,
        "cf_w_pw": cf_w_pw, "na_rpb": na_rpb, "w_out": w_out, "g_post_mix": g_post_mix,
        "g_pre_mlp": g_pre_mlp, "w_mlp1": w_mlp1, "w_mlp2": w_mlp2, "g_post_mlp": g_post_mlp})

    n_rows = SUBLANES
    assert b + 1 <= n_rows and d == D_MODEL and l == GRID_W * GRID_W, (b, l, d)
    cs = jnp.concatenate([c, c_ctx[None, :], jnp.zeros((n_rows - b - 1, d), c.dtype)], axis=0)
    mods = _modulation(cs, w_mod, b_mod).reshape(depth, n_rows, N_MOD, d)
    ctx_row = b
    tabs_lat = _rope_tables(l, True)
    tabs_ctx = _rope_tables(lc, False)

    tm = 1024
    tq = 1024
    xc = ctx
    for i in range(depth):
        last = i == depth - 1
        y_loc, q, k, v, nq, nk, nv = _inproj(x, mods, None, wts, i, tabs_lat, tm)
        yc_loc, q_c, k_c, v_c, nq_c, nk_c, nv_c = _inproj(xc, mods, ctx_row, wts, i, tabs_ctx, lc)

        y_mla = _mla(q, k, v, k_c, v_c, tq)
        y_na = _na(nq, nk, nv, nk_c, nv_c, wts["na_delta"], i)
        x = _outmlp(x, y_loc, y_mla, y_na, mods, None, wts, i, tm)

        if not last:
            yc_mla = _mla(q_c, None, None, k_c, v_c, lc)
            yc_na = _na(nq_c, None, None, nk_c, nv_c, None, i)
            xc = _outmlp(xc, yc_loc, yc_mla, yc_na, mods, ctx_row, wts, i, lc)
    return x
```

```python
import functools

import jax
import jax.numpy as jnp
import numpy as np
from jax import lax
from jax.experimental import pallas as pl
from jax.experimental.pallas import tpu as pltpu

D_MODEL = 1024
GRID_W = 64
GROUP_W = D_MODEL // 4
HEAD_DIM = 64
N_HEADS = GROUP_W // HEAD_DIM
SC_KERNEL = 3
MLA_Q_RANK = 256
MLA_KV_RANK = 128
MLA_NOPE = 64
MLA_ROPE = 32
MLA_V = 64
MLA_SCALE = (MLA_NOPE + MLA_ROPE) ** -0.5
CF_KERNEL = 31
NA_ROWS = 8
NA_COLS = 16
NA_SCALE = HEAD_DIM ** -0.5
ROPE_BASE = 10000.0
MLP_HIDDEN = 4 * D_MODEL
N_MOD = 6
EPS = 1e-6

SC_END = 3 * GROUP_W
MLA_Q_END = SC_END + MLA_Q_RANK
MLA_KV_END = MLA_Q_END + MLA_KV_RANK + MLA_ROPE
CF_END = MLA_KV_END + 2 * GROUP_W
NA_Q_END = CF_END + GROUP_W
P_IN = NA_Q_END + 2 * GROUP_W

LANES = 128
SUBLANES = 8
HEAD_PAD = LANES
MLA_PAD = N_HEADS * HEAD_PAD

NA_QROWS = 4
NA_WROWS = NA_QROWS + NA_ROWS
NA_TQ = NA_QROWS * GRID_W
NA_TK = NA_WROWS * GRID_W
NA_BLOCKS_PER_STEP = 4
HALO = 16
CONV_ROWS = 512
INPROJ_SUB = 512
MASK_VALUE = -1e30
MLA_KCHUNK = 2048
MLA_HEADS_PER_STEP = 4
LOG2E = 1.4426950408889634
MLP_CHUNK = 1024
MOD_BLOCK = 2048

VMEM_LIMIT = 56 * 1024 * 1024


def _cparams(n_axes):
    return pltpu.CompilerParams(dimension_semantics=("arbitrary",) * n_axes,
                                vmem_limit_bytes=VMEM_LIMIT)


def _layer_spec(arr, layer, single_buffer=False):
    nd = arr.ndim - 1
    if nd == 1:
        return pl.BlockSpec(arr.shape, lambda *_: (0, 0))
    kwargs = {"pipeline_mode": pl.Buffered(1)} if single_buffer else {}
    return pl.BlockSpec((None,) + arr.shape[1:], lambda *_: (layer,) + (0,) * nd, **kwargs)


def _mod_spec(mods, layer, row):
    if row is None:
        return pl.BlockSpec((None, None) + mods.shape[2:], lambda i, j: (layer, i, 0, 0))
    return pl.BlockSpec((None, None) + mods.shape[2:], lambda i, j: (layer, row, 0, 0))


def _rms(x):
    return x * lax.rsqrt(jnp.mean(x * x, axis=-1, keepdims=True) + EPS)


def _dot(a, b):
    return jnp.dot(a, b, preferred_element_type=jnp.float32)


def _dot_nt(a, b):
    return lax.dot_general(a, b, (((1,), (1,)), ((), ())), preferred_element_type=jnp.float32)


def _mod_kernel(cs_ref, w_ref, b_ref, o_ref):
    cs = cs_ref[...]
    a = (cs * jax.nn.sigmoid(cs)).astype(jnp.bfloat16)
    o_ref[0] = _dot(a, w_ref[0].astype(jnp.bfloat16)) + b_ref[0]


def _modulation(cs, w_mod, b_mod):
    depth, d, n = w_mod.shape
    bn = MOD_BLOCK
    return pl.pallas_call(
        _mod_kernel,
        grid=(depth, n // bn),
        in_specs=[pl.BlockSpec(cs.shape, lambda l, j: (0, 0)),
                  pl.BlockSpec((1, d, bn), lambda l, j: (l, 0, j)),
                  pl.BlockSpec((1, 1, bn), lambda l, j: (l, 0, j))],
        out_specs=pl.BlockSpec((1, cs.shape[0], bn), lambda l, j: (l, 0, j)),
        out_shape=jax.ShapeDtypeStruct((depth, cs.shape[0], n), jnp.float32),
        compiler_params=_cparams(2), name="modulation",
    )(cs, w_mod, b_mod.reshape(depth, 1, n))


def _inproj_kernel(x_ref, xp_ref, xn_ref, mod_ref, gpre_ref, wsc_in_ref, wmla_in_ref, wcf_in_ref, wna_in_ref,
                   gq_ref, wuq_ref, gkv_ref, wukv_ref,
                   cq_ref, sq_ref, tk_ref, wsc_ref, wdw_ref, bdw_ref, lng_ref, lnb_ref, wpw_ref,
                   yloc_ref, q_ref, k_ref, v_ref, nq_ref, nk_ref, nv_ref,
                   h_ext, ext_sc, ext_cf, shifted, conv_out, *, tm, sub, layer):
    j = pl.program_id(1)
    this = slice(layer, layer + 1)
    sh1 = mod_ref[0:1, :]
    sc1 = mod_ref[1:2, :]
    gain = gpre_ref[this, :] * (1.0 + sc1)

    def modulated(x):
        return (_rms(x) * gain + sh1).astype(jnp.bfloat16)

    n_sub = tm // sub
    n_ext = sub + 2 * HALO
    for t in range(n_sub):
        r0 = t * sub
        rows = slice(r0, r0 + sub)
        x_prev = xp_ref[0] if t == 0 else x_ref[0, r0 - HALO:r0, :]
        x_next = xn_ref[0] if t == n_sub - 1 else x_ref[0, r0 + sub:r0 + sub + HALO, :]
        h_ext[t, 0:HALO, :] = modulated(x_prev)
        h_ext[t, HALO:HALO + sub, :] = modulated(x_ref[0, rows, :])
        h_ext[t, HALO + sub:n_ext, :] = modulated(x_next)
        h_all = h_ext[t]
        h_main = h_ext[t, HALO:HALO + sub, :]
        u_cf = _dot(h_all, wcf_in_ref[...])
        u_sc = _dot(h_all, wsc_in_ref[...])
        u_mla = _dot(h_main, wmla_in_ref[...])
        u_na = _dot(h_main, wna_in_ref[...])

        row = lax.broadcasted_iota(jnp.int32, (n_ext, 1), 0)
        lo_row = jnp.where(j > 0, 0, HALO) if t == 0 else 0
        hi_row = jnp.where(j < pl.num_programs(1) - 1, n_ext, sub + HALO) if t == n_sub - 1 else n_ext
        in_seq = (row >= lo_row) & (row < hi_row)
        ext_sc[t] = jnp.where(in_seq, u_sc[:, 2 * GROUP_W:3 * GROUP_W] * u_sc[:, 0:GROUP_W], 0.0)
        ext_cf[t] = jnp.where(in_seq, u_cf[:, 0:GROUP_W] * jax.nn.sigmoid(u_cf[:, GROUP_W:2 * GROUP_W]), 0.0)

        acc = jnp.zeros((sub, GROUP_W), jnp.float32)
        for k in range(SC_KERNEL):
            off = HALO - SC_KERNEL // 2 + k
            acc = acc + ext_sc[t, off:off + sub, :] * wsc_ref[k:k + 1, :]
        y_sc = u_sc[HALO:HALO + sub, GROUP_W:2 * GROUP_W] * acc

        base = HALO - CF_KERNEL // 2
        s_len = shifted.shape[2]
        for r in range(1, SUBLANES):
            shifted[t, r] = ext_cf[t, r:r + s_len, :]
        chunk = min(CONV_ROWS, sub)
        for c0 in range(0, sub, chunk):
            acc = jnp.zeros((chunk, GROUP_W), jnp.float32)
            for k in range(CF_KERNEL):
                r, lo = (base + k) % SUBLANES, (base + k) // SUBLANES * SUBLANES + c0
                src = ext_cf[t, lo:lo + chunk, :] if r == 0 else shifted[t, r, lo:lo + chunk, :]
                acc = acc + src * wdw_ref[k:k + 1, :]
            conv_out[t, c0:c0 + chunk, :] = acc
        y = conv_out[t] + bdw_ref[this, :]
        mu = jnp.mean(y, axis=-1, keepdims=True)
        yc = y - mu
        var = jnp.mean(yc * yc, axis=-1, keepdims=True)
        z = (yc * lax.rsqrt(var + EPS)) * lng_ref[this, :] + lnb_ref[this, :]
        z = z * jax.nn.sigmoid(z)
        y_cf = _dot(z.astype(jnp.bfloat16), wpw_ref[...])
        yloc_ref[0, rows, 0:GROUP_W] = y_sc.astype(jnp.bfloat16)
        yloc_ref[0, rows, GROUP_W:2 * GROUP_W] = y_cf.astype(jnp.bfloat16)

        nq_ref[0, rows, :] = u_na[:, 0:GROUP_W].astype(jnp.bfloat16)
        nk_ref[0, rows, :] = u_na[:, GROUP_W:2 * GROUP_W].astype(jnp.bfloat16)
        nv_ref[0, rows, :] = u_na[:, 2 * GROUP_W:3 * GROUP_W].astype(jnp.bfloat16)

        cqn = (_rms(u_mla[:, 0:MLA_Q_RANK]) * gq_ref[this, :]).astype(jnp.bfloat16)
        qf = _dot(cqn, wuq_ref[...])
        cq_t = cq_ref[rows, :]
        sq_t = sq_ref[rows, :]
        for hd in range(N_HEADS):
            lo = hd * HEAD_PAD
            qh = qf[:, lo:lo + HEAD_PAD] * cq_t + qf[:, MLA_PAD + lo:MLA_PAD + lo + HEAD_PAD] * sq_t
            q_ref[0, rows, lo:lo + HEAD_PAD] = qh.astype(jnp.bfloat16)

        ckv_lo = MLA_Q_RANK
        rope_lo = MLA_Q_RANK + MLA_KV_RANK
        ckvn = (_rms(u_mla[:, ckv_lo:rope_lo]) * gkv_ref[this, :]).astype(jnp.bfloat16)
        kvf = _dot(ckvn, wukv_ref[...])
        g = u_mla[:, rope_lo:rope_lo + LANES] * tk_ref[rows, :]
        lane = lax.broadcasted_iota(jnp.int32, g.shape, 1)
        rot = pltpu.roll(g, 2 * MLA_ROPE, axis=1) + g
        kr = jnp.where((lane >= MLA_NOPE) & (lane < MLA_NOPE + MLA_ROPE), rot, 0.0)
        for hd in range(N_HEADS):
            lo = hd * HEAD_PAD
            k_ref[0, rows, lo:lo + HEAD_PAD] = (kvf[:, lo:lo + HEAD_PAD] + kr).astype(jnp.bfloat16)
            vh = kvf[:, MLA_PAD + lo:MLA_PAD + lo + HEAD_PAD]
            v_ref[0, rows, lo:lo + HEAD_PAD] = jnp.where(lane == MLA_V, 1.0, vh).astype(jnp.bfloat16)


def _inproj(x, mods, mod_row, wts, layer, tabs, tm):
    b, l, d = x.shape
    grid = (b, l // tm)
    hb = tm // HALO
    nhb = l // HALO
    tok = lambda w: pl.BlockSpec((1, tm, w), lambda i, j: (i, j, 0))
    prev = pl.BlockSpec((1, HALO, d), lambda i, j: (i, jnp.maximum(j * hb - 1, 0), 0))
    nxt = pl.BlockSpec((1, HALO, d), lambda i, j: (i, jnp.minimum((j + 1) * hb, nhb - 1), 0))
    tab = pl.BlockSpec((tm, LANES), lambda i, j: (j, 0))
    f32, bf16 = jnp.float32, jnp.bfloat16
    widths = [2 * GROUP_W, MLA_PAD, MLA_PAD, MLA_PAD, GROUP_W, GROUP_W, GROUP_W]
    names = ["g_pre_mix", "w_in_sc", "w_in_mla", "w_in_cf", "w_in_na", "g_q", "w_uq", "g_kv", "w_ukv"]
    conv_names = ["w_sc", "cf_w_dw", "cf_b_dw", "cf_ln_g", "cf_ln_b", "cf_w_pw"]
    sub = min(tm, INPROJ_SUB)
    n_sub = tm // sub
    assert l % tm == 0 and tm % sub == 0 and sub % min(CONV_ROWS, sub) == 0 and tm % HALO == 0, (l, tm, sub)
    s_len = (HALO - CF_KERNEL // 2 + CF_KERNEL - 1) // SUBLANES * SUBLANES + sub
    return pl.pallas_call(
        functools.partial(_inproj_kernel, tm=tm, sub=sub, layer=layer),
        grid=grid,
        in_specs=[tok(d), prev, nxt, _mod_spec(mods, layer, mod_row)]
                 + [_layer_spec(wts[n], layer) for n in names] + [tab, tab, tab]
                 + [_layer_spec(wts[n], layer) for n in conv_names],
        out_specs=[tok(w) for w in widths],
        out_shape=[jax.ShapeDtypeStruct((b, l, w), bf16) for w in widths],
        scratch_shapes=[pltpu.VMEM((n_sub, sub + 2 * HALO, d), bf16),
                        pltpu.VMEM((n_sub, sub + 2 * HALO, GROUP_W), f32),
                        pltpu.VMEM((n_sub, sub + 2 * HALO, GROUP_W), f32),
                        pltpu.VMEM((n_sub, SUBLANES, s_len, GROUP_W), f32),
                        pltpu.VMEM((n_sub, sub, GROUP_W), f32)],
        compiler_params=_cparams(2), name="inproj",
    )(x, x, x, mods, *[wts[n] for n in names], tabs["cq"], tabs["sq"], tabs["tk"],
      *[wts[n] for n in conv_names])


def _mla_kernel(*refs, has_lat):
    if has_lat:
        q_ref, kl_ref, vl_ref, kc_ref, vc_ref, o_ref = refs
    else:
        q_ref, kc_ref, vc_ref, o_ref = refs
    chunks = [(kc_ref, vc_ref, 0, kc_ref.shape[1])]
    if has_lat:
        n_lat = kl_ref.shape[1]
        chunks += [(kl_ref, vl_ref, lo, MLA_KCHUNK) for lo in range(0, n_lat, MLA_KCHUNK)]
    c2 = MLA_SCALE * LOG2E
    outs = []
    for hd in range(MLA_HEADS_PER_STEP):
        lanes = slice(hd * HEAD_PAD, (hd + 1) * HEAD_PAD)
        q = q_ref[0, :, lanes]
        m = o = None
        for k_ref, v_ref, lo, n in chunks:
            s = _dot_nt(q, k_ref[0, lo:lo + n, lanes])
            m_chunk = jnp.max(s, axis=-1, keepdims=True)
            m_new = m_chunk if m is None else jnp.maximum(m, m_chunk)
            p = jnp.exp2((s - m_new) * c2).astype(jnp.bfloat16)
            pv = _dot(p, v_ref[0, lo:lo + n, lanes])
            o = pv if o is None else o * jnp.exp2((m - m_new) * c2) + pv
            m = m_new
        outs.append(o * (1.0 / o[:, MLA_V:MLA_V + 1]))
    lane = lax.broadcasted_iota(jnp.int32, outs[0].shape, 1)
    for pair in range(MLA_HEADS_PER_STEP // 2):
        packed = jnp.where(lane < MLA_V, outs[2 * pair], pltpu.roll(outs[2 * pair + 1], MLA_V, axis=1))
        o_ref[0, :, pair * HEAD_PAD:(pair + 1) * HEAD_PAD] = packed.astype(jnp.bfloat16)


def _mla(q, k_lat, v_lat, k_ctx, v_ctx, tq):
    b, lq, _ = q.shape
    lc = k_ctx.shape[1]
    has_lat = k_lat is not None
    hw = MLA_HEADS_PER_STEP * HEAD_PAD
    assert lq % tq == 0 and (not has_lat or k_lat.shape[1] % MLA_KCHUNK == 0), (lq, tq)
    grid = (b, N_HEADS // MLA_HEADS_PER_STEP, lq // tq)
    qspec = pl.BlockSpec((1, tq, hw), lambda i, h, j: (i, j, h))
    kvspec = lambda n: pl.BlockSpec((1, n, hw), lambda i, h, j: (i, 0, h))
    in_specs = [qspec]
    args = [q]
    if has_lat:
        in_specs += [kvspec(k_lat.shape[1])] * 2
        args += [k_lat, v_lat]
    in_specs += [kvspec(lc)] * 2
    args += [k_ctx, v_ctx]
    return pl.pallas_call(
        functools.partial(_mla_kernel, has_lat=has_lat),
        grid=grid,
        in_specs=in_specs,
        out_specs=pl.BlockSpec((1, tq, MLA_HEADS_PER_STEP * MLA_V), lambda i, h, j: (i, j, h)),
        out_shape=jax.ShapeDtypeStruct((b, lq, N_HEADS * MLA_V), jnp.bfloat16),
        compiler_params=_cparams(3), name="mla_attn",
    )(*args)


def _na_kernel(*refs, has_local):
    if has_local:
        q_ref, k_ref, v_ref, kc_ref, vc_ref, delta_ref, o_ref, bias_ref, planes_ref = refs

        @pl.when((pl.program_id(0) == 0) & (pl.program_id(1) == 0))
        def _():
            w = GRID_W
            lane = lax.broadcasted_iota(jnp.int32, (w, 2 * w), 1)
            left = lane < w
            qc = lax.broadcasted_iota(jnp.int32, (w, 2 * w), 0)
            kc = lane & (w - 1)
            cstart = jnp.clip(qc - NA_COLS // 2, 0, w - NA_COLS)
            in_cols = (kc >= cstart) & (kc < cstart + NA_COLS)
            for hd, o in np.ndindex(N_HEADS, NA_MASKED_PLANE):
                vec = jnp.broadcast_to(delta_ref[hd, o:o + 1, :], (w, 2 * w))
                lo_half = pltpu.roll(vec, w + 1, axis=1, stride=1, stride_axis=0)
                hi_half = pltpu.roll(vec, 1, axis=1, stride=1, stride_axis=0)
                planes_ref[hd, o] = jnp.where(in_cols, jnp.where(left, lo_half, hi_half), MASK_VALUE)
            for hd in range(N_HEADS):
                planes_ref[hd, NA_MASKED_PLANE] = jnp.full((w, 2 * w), MASK_VALUE, jnp.float32)
            for c, qi, kp in np.ndindex(3, NA_QROWS, NA_WROWS // 2):
                pa, pb = (int(v) for v in _NA_PLANE[c, qi, 2 * kp:2 * kp + 2])
                for hd in range(N_HEADS):
                    blk2 = planes_ref[hd, pa] if pa == pb else jnp.where(left, planes_ref[hd, pa],
                                                                         planes_ref[hd, pb])
                    bias_ref[c, hd, qi * GRID_W:(qi + 1) * GRID_W,
                             2 * kp * GRID_W:2 * (kp + 1) * GRID_W] = blk2

    else:
        q_ref, kc_ref, vc_ref, o_ref = refs
    kc = kc_ref[0]
    vc = vc_ref[0]
    lane = lax.broadcasted_iota(jnp.int32, (NA_TQ, q_ref.shape[2]), 1)
    c2 = NA_SCALE * LOG2E
    n_blocks = q_ref.shape[1] // NA_TQ
    for sb in range(n_blocks):
        rows = slice(sb * NA_TQ, (sb + 1) * NA_TQ)
        if has_local:
            blk = pl.program_id(1) * n_blocks + sb
            nblk = pl.num_programs(1) * n_blocks
            wrow = jnp.clip(blk * NA_QROWS - NA_ROWS // 2, 0, GRID_W - NA_WROWS)
            start = pl.multiple_of(wrow * GRID_W, GRID_W)
            case = jnp.where(blk == 0, 0, jnp.where(blk == nblk - 1, 2, 1))
            kwin = k_ref[0, pl.ds(start, NA_TK), :]
            vwin = v_ref[0, pl.ds(start, NA_TK), :]
        q = q_ref[0, rows, :]
        out = jnp.zeros(q.shape, jnp.float32)
        for hd in range(N_HEADS):
            in_head = (lane >= hd * HEAD_DIM) & (lane < (hd + 1) * HEAD_DIM)
            qm = jnp.where(in_head, q, jnp.zeros_like(q))
            s_c = _dot_nt(qm, kc) * c2
            m = jnp.max(s_c, axis=-1, keepdims=True)
            if has_local:
                s_l = _dot_nt(qm, kwin) * c2 + bias_ref[case, hd]
                m = jnp.maximum(m, jnp.max(s_l, axis=-1, keepdims=True))
            p_c = jnp.exp2(s_c - m)
            den = jnp.sum(p_c, axis=-1, keepdims=True)
            o = _dot(p_c.astype(jnp.bfloat16), vc)
            if has_local:
                p_l = jnp.exp2(s_l - m)
                den = den + jnp.sum(p_l, axis=-1, keepdims=True)
                o = o + _dot(p_l.astype(jnp.bfloat16), vwin)
            out = jnp.where(in_head, o * (1.0 / den), out)
        o_ref[0, rows, :] = out.astype(jnp.bfloat16)


def _na(q, k, v, k_ctx, v_ctx, delta, layer):
    b, l, w = q.shape
    lc = k_ctx.shape[1]
    has_local = k is not None
    tq = NA_TQ * NA_BLOCKS_PER_STEP if has_local else NA_TQ
    assert l % tq == 0 and (not has_local or l == GRID_W * GRID_W), (l, tq)
    grid = (b, l // tq)
    qspec = pl.BlockSpec((1, tq, w), lambda i, j: (i, j, 0))
    full = lambda n: pl.BlockSpec((1, n, w), lambda i, j: (i, 0, 0))
    in_specs = [qspec]
    args = [q]
    if has_local:
        in_specs += [full(l), full(l)]
        args += [k, v]
    in_specs += [full(lc), full(lc)]
    args += [k_ctx, v_ctx]
    scratch = []
    if has_local:
        in_specs += [_layer_spec(delta, layer)]
        args += [delta]
        scratch = [pltpu.VMEM((3, N_HEADS, NA_TQ, NA_TK), jnp.float32),
                   pltpu.VMEM((N_HEADS, NA_MASKED_PLANE + 1, GRID_W, 2 * GRID_W), jnp.float32)]
    return pl.pallas_call(
        functools.partial(_na_kernel, has_local=has_local),
        grid=grid,
        in_specs=in_specs,
        out_specs=qspec,
        out_shape=jax.ShapeDtypeStruct((b, l, w), jnp.bfloat16),
        scratch_shapes=scratch,
        compiler_params=_cparams(2), name="na_attn",
    )(*args)


NA_MASKED_PLANE = 2 * NA_ROWS - 1


def _na_plane_index():
    rows = GRID_W
    idx = np.full((3, NA_QROWS, NA_WROWS), NA_MASKED_PLANE, np.int32)
    for c, r0 in enumerate((0, NA_QROWS, rows - NA_QROWS)):
        wrow = min(max(r0 - NA_ROWS // 2, 0), rows - NA_WROWS)
        for qi in range(NA_QROWS):
            r = r0 + qi
            band = min(max(r - NA_ROWS // 2, 0), rows - NA_ROWS)
            for kj in range(NA_WROWS):
                kr = wrow + kj
                if band <= kr < band + NA_ROWS:
                    idx[c, qi, kj] = kr - r + (NA_ROWS - 1)
    return idx


_NA_PLANE = _na_plane_index()


def _na_delta_table(rpb):
    w = GRID_W
    delta = np.arange(2 * w) - (w - 1)
    return rpb[..., np.clip(delta + (NA_COLS - 1), 0, 2 * NA_COLS - 2)] * LOG2E


def _outmlp_kernel(x_ref, yloc_ref, ymla_ref, yna_ref, mod_ref, gpm_ref, gpre_ref, gpost_ref,
                   wout_ref, w1_ref, w2_ref, o_ref, *, layer):
    ga1 = mod_ref[2:3, :]
    sh2 = mod_ref[3:4, :]
    sc2 = mod_ref[4:5, :]
    ga2 = mod_ref[5:6, :]
    this = slice(layer, layer + 1)
    g_mix = gpm_ref[this, :] * ga1
    g_mlp_in = gpre_ref[this, :] * (1.0 + sc2)
    g_mlp_out = gpost_ref[this, :] * ga2
    x = x_ref[0]
    ycat = jnp.concatenate([yloc_ref[0, :, 0:GROUP_W], ymla_ref[0], yloc_ref[0, :, GROUP_W:2 * GROUP_W],
                            yna_ref[0]], axis=-1)
    y = _dot(ycat, wout_ref[...])
    x1 = x + _rms(y) * g_mix
    hm = (_rms(x1) * g_mlp_in + sh2).astype(jnp.bfloat16)
    acc = jnp.zeros(x.shape, jnp.float32)
    for c in range(MLP_HIDDEN // MLP_CHUNK):
        lo = c * MLP_CHUNK
        hj = jnp.maximum(_dot(hm, w1_ref[:, lo:lo + MLP_CHUNK]), 0.0)
        acc = acc + _dot((hj * hj).astype(jnp.bfloat16), w2_ref[lo:lo + MLP_CHUNK, :])
    o_ref[0] = x1 + _rms(acc) * g_mlp_out


def _outmlp(x, yloc, ymla, yna, mods, mod_row, wts, layer, tm):
    b, l, d = x.shape
    assert l % tm == 0, (l, tm)
    grid = (b, l // tm)
    tok = lambda w: pl.BlockSpec((1, tm, w), lambda i, j: (i, j, 0))
    gains = ["g_post_mix", "g_pre_mlp", "g_post_mlp"]
    mats = ["w_out", "w_mlp1", "w_mlp2"]
    return pl.pallas_call(
        functools.partial(_outmlp_kernel, layer=layer),
        grid=grid,
        in_specs=[tok(d), tok(2 * GROUP_W), tok(GROUP_W), tok(GROUP_W), _mod_spec(mods, layer, mod_row)]
                 + [_layer_spec(wts[n], layer) for n in gains]
                 + [_layer_spec(wts[n], layer, single_buffer=True) for n in mats],
        out_specs=tok(d),
        out_shape=jax.ShapeDtypeStruct((b, l, d), jnp.float32),
        compiler_params=_cparams(2), name="outproj_mlp",
    )(x, yloc, ymla, yna, mods, *[wts[n] for n in gains], *[wts[n] for n in mats])


def _swap_halves_signed(w):
    q = MLA_ROPE // 4
    x1, x2, x3, x4 = (w[..., i * q:(i + 1) * q] for i in range(4))
    return jnp.concatenate([-x2, x1, -x4, x3], axis=-1)


def _prep_weights(p):
    bf16 = jnp.bfloat16
    w_in = p["w_in"]
    depth, d, _ = w_in.shape
    k_rope = w_in[..., MLA_Q_END + MLA_KV_RANK:MLA_KV_END]
    z32 = jnp.zeros((depth, d, MLA_ROPE), w_in.dtype)
    w_in_sc = w_in[..., :SC_END].astype(bf16)
    w_in_mla = jnp.concatenate([w_in[..., SC_END:MLA_Q_END + MLA_KV_RANK],
                                k_rope, z32, _swap_halves_signed(k_rope), z32], axis=-1).astype(bf16)
    w_in_cf = w_in[..., MLA_KV_END:CF_END].astype(bf16)
    w_in_na = w_in[..., CF_END:].astype(bf16)

    wq = p["w_uq"].reshape(depth, MLA_Q_RANK, N_HEADS, MLA_NOPE + MLA_ROPE)
    zq = jnp.zeros((depth, MLA_Q_RANK, N_HEADS, HEAD_PAD - MLA_NOPE - MLA_ROPE), wq.dtype)
    zn = jnp.zeros((depth, MLA_Q_RANK, N_HEADS, MLA_NOPE), wq.dtype)
    q_main = jnp.concatenate([wq, zq], axis=-1).reshape(depth, MLA_Q_RANK, MLA_PAD)
    q_swap = jnp.concatenate([zn, _swap_halves_signed(wq[..., MLA_NOPE:]), zq], axis=-1)
    w_uq_p = jnp.concatenate([q_main, q_swap.reshape(depth, MLA_Q_RANK, MLA_PAD)], axis=-1).astype(bf16)

    wkv = p["w_ukv"].reshape(depth, MLA_KV_RANK, N_HEADS, MLA_NOPE + MLA_V)
    zk = jnp.zeros((depth, MLA_KV_RANK, N_HEADS, HEAD_PAD - MLA_NOPE), wkv.dtype)
    k_main = jnp.concatenate([wkv[..., :MLA_NOPE], zk], axis=-1).reshape(depth, MLA_KV_RANK, MLA_PAD)
    v_main = jnp.concatenate([wkv[..., MLA_NOPE:], zk], axis=-1).reshape(depth, MLA_KV_RANK, MLA_PAD)
    w_ukv_p = jnp.concatenate([k_main, v_main], axis=-1).astype(bf16)

    out = {name: p[name] for name in ("g_pre_mix", "g_q", "g_kv", "w_sc", "cf_w_dw", "cf_b_dw", "cf_ln_g",
                                      "cf_ln_b", "g_post_mix", "g_pre_mlp", "g_post_mlp")}
    out.update({
        "w_in_sc": w_in_sc, "w_in_mla": w_in_mla, "w_in_cf": w_in_cf, "w_in_na": w_in_na,
        "w_uq": w_uq_p, "w_ukv": w_ukv_p, "cf_w_pw": p["cf_w_pw"].astype(bf16),
        "w_out": p["w_out"].astype(bf16), "w_mlp1": p["w_mlp1"].astype(bf16),
        "w_mlp2": p["w_mlp2"].astype(bf16), "na_delta": _na_delta_table(p["na_rpb"]),
    })
    return out


def _rope_tables(l, rotate):
    f32 = np.float32
    ones = np.ones((l, MLA_NOPE), f32)
    zpad = np.zeros((l, HEAD_PAD - MLA_NOPE - MLA_ROPE), f32)
    z32 = np.zeros((l, MLA_ROPE), f32)
    if rotate:
        pos = np.arange(l)
        n_freq = MLA_ROPE // 4
        inv_freq = (f32(ROPE_BASE) ** (-np.arange(n_freq, dtype=f32) / f32(n_freq))).astype(f32)
        ang_row = (pos // GRID_W).astype(f32)[:, None] * inv_freq
        ang_col = (pos % GRID_W).astype(f32)[:, None] * inv_freq
        cos = np.concatenate([np.cos(ang_row)] * 2 + [np.cos(ang_col)] * 2, axis=-1).astype(f32)
        sin = np.concatenate([np.sin(ang_row)] * 2 + [np.sin(ang_col)] * 2, axis=-1).astype(f32)
    else:
        cos = np.ones((l, MLA_ROPE), f32)
        sin = z32
    return {"cq": jnp.asarray(np.concatenate([ones, cos, zpad], axis=-1)),
            "sq": jnp.asarray(np.concatenate([np.zeros_like(ones), sin, zpad], axis=-1)),
            "tk": jnp.asarray(np.concatenate([cos, z32, sin, z32], axis=-1))}


def kernel(x, c, ctx, c_ctx, w_mod, b_mod, g_pre_mix, w_in, w_sc, g_q, w_uq, g_kv, w_ukv, cf_w_dw,
           cf_b_dw, cf_ln_g, cf_ln_b, cf_w_pw, na_rpb, w_out, g_post_mix, g_pre_mlp, w_mlp1, w_mlp2,
           g_post_mlp):
    b, l, d = x.shape
    lc = ctx.shape[1]
    depth = w_mod.shape[0]
    wts = _prep_weights({
        "g_pre_mix": g_pre_mix, "w_in": w_in, "w_sc": w_sc, "g_q": g_q, "w_uq": w_uq, "g_kv": g_kv,
        "w_ukv": w_ukv, "cf_w_dw": cf_w_dw, "cf_b_dw": cf_b_dw, "cf_ln_g": cf_ln_g, "cf_ln_b": cf_ln_b,
        "cf_w_pw": cf_w_pw, "na_rpb": na_rpb, "w_out": w_out, "g_post_mix": g_post_mix,
        "g_pre_mlp": g_pre_mlp, "w_mlp1": w_mlp1, "w_mlp2": w_mlp2, "g_post_mlp": g_post_mlp})

    n_rows = SUBLANES
    assert b + 1 <= n_rows and d == D_MODEL and l == GRID_W * GRID_W, (b, l, d)
    cs = jnp.concatenate([c, c_ctx[None, :], jnp.zeros((n_rows - b - 1, d), c.dtype)], axis=0)
    mods = _modulation(cs, w_mod, b_mod).reshape(depth, n_rows, N_MOD, d)
    ctx_row = b
    tabs_lat = _rope_tables(l, True)
    tabs_ctx = _rope_tables(lc, False)

    tm = 1024
    tq = 1024
    xc = ctx
    for i in range(depth):
        last = i == depth - 1
        y_loc, q, k, v, nq, nk, nv = _inproj(x, mods, None, wts, i, tabs_lat, tm)
        yc_loc, q_c, k_c, v_c, nq_c, nk_c, nv_c = _inproj(xc, mods, ctx_row, wts, i, tabs_ctx, lc)

        y_mla = _mla(q, k, v, k_c, v_c, tq)
        y_na = _na(nq, nk, nv, nk_c, nv_c, wts["na_delta"], i)
        x = _outmlp(x, y_loc, y_mla, y_na, mods, None, wts, i, tm)

        if not last:
            yc_mla = _mla(q_c, None, None, k_c, v_c, lc)
            yc_na = _na(nq_c, None, None, nk_c, nv_c, None, i)
            xc = _outmlp(xc, yc_loc, yc_mla, yc_na, mods, ctx_row, wts, i, lc)
    return x
```
